```python
import jax, jax.numpy as jnp
from jax import lax
import numpy as np

D_MODEL = 1024
BATCH = 4
SEQ = 4096
DEPTH = 2

HEAD_DIM = 64
N_HEADS = D_MODEL // HEAD_DIM
N_HEADS_A = N_HEADS // 2
N_HEADS_B = N_HEADS - N_HEADS_A
WIDTH_A = N_HEADS_A * HEAD_DIM
WIDTH_B = N_HEADS_B * HEAD_DIM
CHUNK = 128
Q_BLOCK = 128
CONV_WIDTH = 3
CONV_DIM = D_MODEL
D_FF = ((8 * D_MODEL // 3 + 127) // 128) * 128
MACARON_WEIGHT = 0.5
N_SUB = 3
N_EVEN = (DEPTH + 1) // 2
N_ODD = DEPTH // 2
EPS = 1e-6

kernel_name = "hybrid_gmlp_stickbreak_shortconv_macaron_adaln"


def rms_norm(x, g):
    xf = x.astype(jnp.float32)
    y = xf * lax.rsqrt(jnp.mean(xf * xf, axis=-1, keepdims=True) + EPS)
    return (y * g.astype(jnp.float32)).astype(x.dtype)


def modulate(x, g, shift, scale):
    return rms_norm(x, g) * (1 + scale[:, None, :]) + shift[:, None, :]


def swiglu(h, w_gate, w_up, w_down):
    return (jax.nn.silu(h @ w_gate) * (h @ w_up)) @ w_down


def stick_breaking_attention(q, k, v):
    S = q.shape[1]
    scale = HEAD_DIM ** -0.5
    outs = []
    for blk in range(S // Q_BLOCK):
        q0 = blk * Q_BLOCK
        kv_len = q0 + Q_BLOCK
        qb = q[:, q0:kv_len].astype(jnp.float32)
        kb = k[:, :kv_len].astype(jnp.float32)
        z = jnp.einsum('bqhd,bkhd->bhqk', qb, kb) * scale
        t_pos = q0 + jnp.arange(Q_BLOCK)[:, None]
        s_pos = jnp.arange(kv_len)[None, :]
        before = s_pos < t_pos
        log_1mb = jnp.where(before, -jax.nn.softplus(z), 0.0)
        tail = lax.cumsum(log_1mb, axis=3, reverse=True) - log_1mb
        w = jnp.where(before, jnp.exp(jax.nn.log_sigmoid(z) + tail), 0.0)
        outs.append(jnp.einsum('bhqk,bkhd->bqhd', w.astype(v.dtype), v[:, :kv_len]))
    return jnp.concatenate(outs, axis=1)


def gmlp_stickbreak_mixer(h, w_in, vnorm_g, w_s, b_s, w_out):
    Bsz, S, _ = h.shape
    proj = h @ w_in
    uv_a, qkv_b = proj[..., :2 * WIDTH_A], proj[..., 2 * WIDTH_A:]
    u, v = jnp.split(jax.nn.gelu(uv_a, approximate=False), 2, axis=-1)
    u = u.reshape(Bsz, S, N_HEADS_A, HEAD_DIM)
    v = rms_norm(v.reshape(Bsz, S, N_HEADS_A, HEAD_DIM), vnorm_g.reshape(N_HEADS_A, HEAD_DIM))
    v = v.reshape(Bsz, S // CHUNK, CHUNK, N_HEADS_A, HEAD_DIM)
    w_causal = jnp.tril(w_s)
    sv = jnp.einsum('hts,bnshd->bnthd', w_causal, v) + b_s.T[:, :, None]
    y_a = u * sv.reshape(Bsz, S, N_HEADS_A, HEAD_DIM)
    q, k, vb = jnp.split(qkv_b, 3, axis=-1)
    q = q.reshape(Bsz, S, N_HEADS_B, HEAD_DIM)
    k = k.reshape(Bsz, S, N_HEADS_B, HEAD_DIM)
    vb = vb.reshape(Bsz, S, N_HEADS_B, HEAD_DIM)
    y_b = stick_breaking_attention(q, k, vb)
    y = jnp.concatenate([y_a.reshape(Bsz, S, WIDTH_A), y_b.reshape(Bsz, S, WIDTH_B)], axis=-1)
    return y @ w_out


def short_conv_mixer(h, w_in, conv_w, w_out):
    b_gate, c_gate, xs = jnp.split(h @ w_in, 3, axis=-1)
    y = lax.conv_general_dilated(
        c_gate * xs, conv_w[:, None, :].astype(xs.dtype),
        window_strides=(1,), padding=((CONV_WIDTH - 1, 0),),
        dimension_numbers=('NWC', 'WIO', 'NWC'), feature_group_count=CONV_DIM)
    return (b_gate * y) @ w_out


def setup_inputs(seed: int = 0) -> dict:
    key = jax.random.key(seed)
    ks = jax.random.split(key, 18)
    f32 = jnp.float32
    D = D_MODEL

    def nrm(k, shape, fan_in):
        return jax.random.normal(k, shape, f32) * (fan_in ** -0.5)

    return {
        "x": jax.random.normal(ks[0], (BATCH, SEQ, D), f32),
        "c": jax.random.normal(ks[1], (BATCH, D), f32),
        "mod_w": nrm(ks[2], (DEPTH, D, 3 * N_SUB * D), D) * 0.5,
        "mod_b": 0.02 * jax.random.normal(ks[3], (DEPTH, 3 * N_SUB * D), f32),
        "norm_g": 1.0 + 0.02 * jax.random.normal(ks[4], (DEPTH, N_SUB, D), f32),
        "ffn_w_gate": nrm(ks[5], (DEPTH, 2, D, D_FF), D),
        "ffn_w_up": nrm(ks[6], (DEPTH, 2, D, D_FF), D),
        "ffn_w_down": nrm(ks[7], (DEPTH, 2, D_FF, D), D_FF),
        "hy_w_in": nrm(ks[8], (N_EVEN, D, 2 * WIDTH_A + 3 * WIDTH_B), D),
        "hy_w_out": nrm(ks[9], (N_EVEN, WIDTH_A + WIDTH_B, D), WIDTH_A + WIDTH_B),
        "gm_vnorm_g": 1.0 + 0.02 * jax.random.normal(ks[10], (N_EVEN, WIDTH_A), f32),
        "gm_w_s": nrm(ks[11], (N_EVEN, N_HEADS_A, CHUNK, CHUNK), CHUNK),
        "gm_b_s": 1.0 + 0.02 * jax.random.normal(ks[12], (N_EVEN, N_HEADS_A, CHUNK), f32),
        "sc_w_in": nrm(ks[13], (N_ODD, D, 3 * CONV_DIM), D),
        "sc_conv_w": nrm(ks[14], (N_ODD, CONV_WIDTH, CONV_DIM), CONV_WIDTH),
        "sc_w_out": nrm(ks[15], (N_ODD, CONV_DIM, D), CONV_DIM),
        "final_norm_g": 1.0 + 0.02 * jax.random.normal(ks[16], (D,), f32),
    }


def reference(x, c, mod_w, mod_b, norm_g, ffn_w_gate, ffn_w_up, ffn_w_down,
              hy_w_in, hy_w_out, gm_vnorm_g, gm_w_s, gm_b_s,
              sc_w_in, sc_conv_w, sc_w_out, final_norm_g):
    cond = jax.nn.silu(c)
    for layer in range(DEPTH):
        mod = cond @ mod_w[layer] + mod_b[layer]
        sh1, sc1, g1, sh2, sc2, g2, sh3, sc3, g3 = jnp.split(mod, 3 * N_SUB, axis=-1)
        h = modulate(x, norm_g[layer, 0], sh1, sc1)
        x = x + MACARON_WEIGHT * g1[:, None, :] * swiglu(
            h, ffn_w_gate[layer, 0], ffn_w_up[layer, 0], ffn_w_down[layer, 0])
        h = modulate(x, norm_g[layer, 1], sh2, sc2)
        i = layer // 2
        if layer % 2 == 0:
            m = gmlp_stickbreak_mixer(h, hy_w_in[i], gm_vnorm_g[i], gm_w_s[i], gm_b_s[i], hy_w_out[i])
        else:
            m = short_conv_mixer(h, sc_w_in[i], sc_conv_w[i], sc_w_out[i])
        x = x + g2[:, None, :] * m
        h = modulate(x, norm_g[layer, 2], sh3, sc3)
        x = x + MACARON_WEIGHT * g3[:, None, :] * swiglu(
            h, ffn_w_gate[layer, 1], ffn_w_up[layer, 1], ffn_w_down[layer, 1])
    return rms_norm(x, final_norm_g)
```

```python
import functools

import jax
import jax.numpy as jnp
from jax import lax
from jax.experimental import pallas as pl
from jax.experimental.pallas import tpu as pltpu

F32 = jnp.float32
BF16 = jnp.bfloat16

HEAD_DIM = 64
CHUNK = 128
EPS = 1e-6
MACARON_WEIGHT = 0.5
N_MOD = 9
MOD_ROWS = 8

LANES = 128
SB_BLOCK = 256
TOKEN_TILE = 512
VMEM_LIMIT = 56 * 1024 * 1024


def _resident(block_shape, index_map):
    return pl.BlockSpec(block_shape, index_map, pipeline_mode=pl.Buffered(1))


def _params(*semantics):
    return pltpu.CompilerParams(dimension_semantics=semantics,
                                vmem_limit_bytes=VMEM_LIMIT)


def _rms(x):
    return lax.rsqrt(jnp.mean(x * x, axis=-1, keepdims=True) + EPS)


def _modulated(x, ng, sc, sh):
    return (x * _rms(x)) * (ng * (1.0 + sc)) + sh


def _silu(x):
    return x * jax.nn.sigmoid(x)


def _mod_kernel(c_ref, w_ref, b_ref, o_ref):
    cond = _silu(c_ref[...]).astype(BF16)
    o_ref[0] = jnp.dot(cond, w_ref[0].astype(BF16),
                       preferred_element_type=F32) + b_ref[0]


def _modulation(c, mod_w, mod_b):
    depth, d, n = mod_w.shape
    bsz = c.shape[0]
    tn = 1024
    c8 = jnp.zeros((MOD_ROWS, d), F32).at[:bsz].set(c)
    return pl.pallas_call(
        _mod_kernel,
        grid=(depth, n // tn),
        in_specs=[
            pl.BlockSpec((MOD_ROWS, d), lambda l, j: (0, 0)),
            pl.BlockSpec((1, d, tn), lambda l, j: (l, 0, j)),
            pl.BlockSpec((1, 1, tn), lambda l, j: (l, 0, j)),
        ],
        out_specs=pl.BlockSpec((1, MOD_ROWS, tn), lambda l, j: (l, 0, j)),
        out_shape=jax.ShapeDtypeStruct((depth, MOD_ROWS, n), F32),
        compiler_params=_params("parallel", "parallel"),
        name="adaln_mod",
    )(c8, mod_w, mod_b.reshape(depth, 1, n))


def _mod_spec(layer, k, d):
    base = layer * MOD_ROWS * N_MOD + k
    return pl.BlockSpec((1, 1, d), lambda b, i: (base + b * N_MOD, 0, 0))


def _ffn_kernel(x_ref, sh_ref, sc_ref, g_ref, ng_ref, wg_ref, wu_ref, wd_ref, fg_ref,
                o_ref, a_ref, *, ff_chunk, final):
    x = x_ref[0]
    h = _modulated(x, ng_ref[0], sc_ref[0], sh_ref[0]).astype(BF16)
    d_ff = a_ref.shape[1]
    for c0 in range(0, d_ff, ff_chunk):
        sl = slice(c0, c0 + ff_chunk)
        g = jnp.dot(h, wg_ref[0, 0, :, sl], preferred_element_type=F32)
        u = jnp.dot(h, wu_ref[0, 0, :, sl], preferred_element_type=F32)
        a_ref[:, sl] = (_silu(g) * u).astype(BF16)
    y = jnp.dot(a_ref[...], wd_ref[0, 0], preferred_element_type=F32)
    out = x + (MACARON_WEIGHT * g_ref[0]) * y
    if final:
        out = (out * _rms(out)) * fg_ref[...]
    o_ref[0] = out


def _ffn(x, mod3, ng3, wg, wu, wd, final_g, *, layer, sub, which, final):
    bsz, s, d = x.shape
    d_ff = wg.shape[-1]
    tm = TOKEN_TILE
    kern = functools.partial(_ffn_kernel, ff_chunk=256, final=final)
    wspec_in = _resident((1, 1, d, d_ff), lambda b, i: (layer, which, 0, 0))
    wspec_out = _resident((1, 1, d_ff, d), lambda b, i: (layer, which, 0, 0))
    return pl.pallas_call(
        kern,
        grid=(bsz, s // tm),
        in_specs=[
            pl.BlockSpec((1, tm, d), lambda b, i: (b, i, 0)),
            _mod_spec(layer, 3 * sub + 0, d),
            _mod_spec(layer, 3 * sub + 1, d),
            _mod_spec(layer, 3 * sub + 2, d),
            pl.BlockSpec((1, 1, d), lambda b, i: (layer * 3 + sub, 0, 0)),
            wspec_in, wspec_in, wspec_out,
            pl.BlockSpec((1, d), lambda b, i: (0, 0)),
        ],
        out_specs=pl.BlockSpec((1, tm, d), lambda b, i: (b, i, 0)),
        out_shape=jax.ShapeDtypeStruct((bsz, s, d), F32),
        scratch_shapes=[pltpu.VMEM((tm, d_ff), BF16)],
        compiler_params=_params("parallel", "parallel"),
        name=f"ffn_l{layer}_{which}",
    )(x, mod3, mod3, mod3, ng3, wg, wu, wd, final_g.reshape(1, d))


def _hyin_kernel(x_ref, sh_ref, sc_ref, ng_ref, wa_ref, wq_ref, wkt_ref, wv_ref, vg_ref,
                 gmat_ref, ws_ref, bs_ref, ya_ref, q_ref, kt_ref, v_ref):
    x = x_ref[0]
    tm = x.shape[0]
    h = _modulated(x, ng_ref[0], sc_ref[0], sh_ref[0]).astype(BF16)

    q = jnp.dot(h, wq_ref[...], preferred_element_type=F32)
    q_ref[0] = (q * (HEAD_DIM ** -0.5)).astype(BF16)
    v_ref[0] = jnp.dot(h, wv_ref[...], preferred_element_type=F32).astype(BF16)
    kt_ref[0] = lax.dot_general(wkt_ref[...], h, (((1,), (1,)), ((), ())),
                                preferred_element_type=F32).astype(BF16)

    uv = jnp.dot(h, wa_ref[...], preferred_element_type=F32)
    uv = 0.5 * uv * (1.0 + lax.erf(uv * (2.0 ** -0.5)))
    wa = uv.shape[1] // 2
    u, v = uv[:, :wa], uv[:, wa:]
    vsq = v * v
    hi = vsq.astype(BF16)
    lo = (vsq - hi.astype(F32)).astype(BF16)
    ssq = (jnp.dot(hi, gmat_ref[...], preferred_element_type=F32)
           + jnp.dot(lo, gmat_ref[...], preferred_element_type=F32))
    vn = (v * lax.rsqrt(ssq * (1.0 / HEAD_DIM) + EPS) * vg_ref[...]).astype(BF16)

    n_pairs = wa // LANES
    r2 = lax.broadcasted_iota(jnp.int32, (2 * CHUNK, CHUNK), 0)
    c2 = lax.broadcasted_iota(jnp.int32, (2 * CHUNK, CHUNK), 1)
    causal = (r2 & (CHUNK - 1)) >= c2
    first_head = lax.broadcasted_iota(jnp.int32, (CHUNK, LANES), 1) < HEAD_DIM
    w_pairs = [jnp.where(causal, ws_ref[p], jnp.zeros_like(ws_ref[p])) for p in range(n_pairs)]
    for ci in range(tm // CHUNK):
        rows = slice(ci * CHUNK, (ci + 1) * CHUNK)
        parts = []
        for p in range(n_pairs):
            vp = vn[rows, p * LANES:(p + 1) * LANES]
            r = jnp.dot(w_pairs[p], vp, preferred_element_type=F32)
            parts.append(jnp.where(first_head, r[:CHUNK], r[CHUNK:]))
        sv = jnp.concatenate(parts, axis=1) + bs_ref[...]
        ya_ref[0, rows, :] = (u[rows] * sv).astype(BF16)


def _hy_in(x, mod3, ng3, wa, wq, wkt, wv, vg, gmat, ws2, bs_tile, *, layer, sub):
    bsz, s, d = x.shape
    tm = TOKEN_TILE
    wa_n, wb = wa.shape[1], wq.shape[1]
    const2 = lambda b, i: (0, 0)
    return pl.pallas_call(
        _hyin_kernel,
        grid=(bsz, s // tm),
        in_specs=[
            pl.BlockSpec((1, tm, d), lambda b, i: (b, i, 0)),
            _mod_spec(layer, 3 * sub + 0, d),
            _mod_spec(layer, 3 * sub + 1, d),
            pl.BlockSpec((1, 1, d), lambda b, i: (layer * 3 + sub, 0, 0)),
            _resident(wa.shape, const2),
            _resident(wq.shape, const2),
            _resident(wkt.shape, const2),
            _resident(wv.shape, const2),
            pl.BlockSpec(vg.shape, const2),
            _resident(gmat.shape, const2),
            _resident(ws2.shape, lambda b, i: (0, 0, 0)),
            pl.BlockSpec(bs_tile.shape, const2),
        ],
        out_specs=[
            pl.BlockSpec((1, tm, wa_n // 2), lambda b, i: (b, i, 0)),
            pl.BlockSpec((1, tm, wb), lambda b, i: (b, i, 0)),
            pl.BlockSpec((1, wb, tm), lambda b, i: (b, 0, i)),
            pl.BlockSpec((1, tm, wb), lambda b, i: (b, i, 0)),
        ],
        out_shape=[
            jax.ShapeDtypeStruct((bsz, s, wa_n // 2), BF16),
            jax.ShapeDtypeStruct((bsz, s, wb), BF16),
            jax.ShapeDtypeStruct((bsz, wb, s), BF16),
            jax.ShapeDtypeStruct((bsz, s, wb), BF16),
        ],
        compiler_params=_params("parallel", "parallel"),
        name="hy_in",
    )(x, mod3, mod3, ng3, wa, wq, wkt, wv, vg, gmat, ws2, bs_tile)


def _sb_kernel(q_ref, kt_ref, v_ref, o_ref, acc_ref, c_ref):
    t = q_ref.shape[1]
    qi = pl.program_id(2)
    q = q_ref[0]
    first_head = lax.broadcasted_iota(jnp.int32, (t, LANES), 1) < HEAD_DIM
    zero = jnp.zeros_like(q)
    q_heads = (jnp.where(first_head, q, zero), jnp.where(first_head, zero, q))
    row = lax.broadcasted_iota(jnp.int32, (t, t), 0)
    col = lax.broadcasted_iota(jnp.int32, (t, t), 1)
    later = (row > col).astype(BF16)
    before = col < row

    acc_ref[...] = jnp.zeros_like(acc_ref)
    c_ref[...] = jnp.zeros_like(c_ref)

    def block(j, diag):
        k0 = pl.multiple_of(j * t, t)
        kt = kt_ref[0, :, pl.ds(k0, t)]
        vv = v_ref[0, pl.ds(k0, t), :]
        for hh in range(2):
            z = jnp.dot(q_heads[hh], kt, preferred_element_type=F32)
            lg = jnp.log1p(jnp.exp(-jnp.abs(z)))
            log_1mb = jnp.minimum(-z, 0.0) - lg
            log_b = jnp.minimum(z, 0.0) - lg
            if diag:
                log_1mb = jnp.where(before, log_1mb, 0.0)
            hi = log_1mb.astype(BF16)
            lo = (log_1mb - hi.astype(F32)).astype(BF16)
            tail = (jnp.dot(hi, later, preferred_element_type=F32)
                    + jnp.dot(lo, later, preferred_element_type=F32)
                    + c_ref[hh])
            w = jnp.exp(log_b + tail)
            if diag:
                w = jnp.where(before, w, 0.0)
            acc_ref[hh] += jnp.dot(w.astype(BF16), vv, preferred_element_type=F32)
            c_ref[hh] += jnp.sum(log_1mb, axis=1, keepdims=True)

    block(qi, True)

    def body(it, carry):
        block(qi - 1 - it, False)
        return carry

    lax.fori_loop(0, qi, body, 0)
    o_ref[0] = jnp.where(first_head, acc_ref[0], acc_ref[1]).astype(o_ref.dtype)


def _stick_breaking(q, kt, v):
    bsz, s, wb = q.shape
    t = SB_BLOCK
    return pl.pallas_call(
        _sb_kernel,
        grid=(bsz, wb // LANES, s // t),
        in_specs=[
            pl.BlockSpec((1, t, LANES), lambda b, p, i: (b, i, p)),
            pl.BlockSpec((1, LANES, s), lambda b, p, i: (b, p, 0)),
            pl.BlockSpec((1, s, LANES), lambda b, p, i: (b, 0, p)),
        ],
        out_specs=pl.BlockSpec((1, t, LANES), lambda b, p, i: (b, i, p)),
        out_shape=jax.ShapeDtypeStruct((bsz, s, wb), BF16),
        scratch_shapes=[pltpu.VMEM((2, t, LANES), F32), pltpu.VMEM((2, t, 1), F32)],
        compiler_params=_params("parallel", "parallel", "parallel"),
        name="stick_breaking",
    )(q, kt, v)


def _hyout_kernel(x_ref, ya_ref, yb_ref, wo_ref, g_ref, o_ref):
    wa = ya_ref.shape[2]
    m = (jnp.dot(ya_ref[0], wo_ref[:wa, :], preferred_element_type=F32)
         + jnp.dot(yb_ref[0], wo_ref[wa:, :], preferred_element_type=F32))
    o_ref[0] = x_ref[0] + g_ref[0] * m


def _hy_out(x, ya, yb, wo, mod3, *, layer, sub):
    bsz, s, d = x.shape
    tm = TOKEN_TILE
    return pl.pallas_call(
        _hyout_kernel,
        grid=(bsz, s // tm),
        in_specs=[
            pl.BlockSpec((1, tm, d), lambda b, i: (b, i, 0)),
            pl.BlockSpec((1, tm, ya.shape[2]), lambda b, i: (b, i, 0)),
            pl.BlockSpec((1, tm, yb.shape[2]), lambda b, i: (b, i, 0)),
            _resident(wo.shape, lambda b, i: (0, 0)),
            _mod_spec(layer, 3 * sub + 2, d),
        ],
        out_specs=pl.BlockSpec((1, tm, d), lambda b, i: (b, i, 0)),
        out_shape=jax.ShapeDtypeStruct((bsz, s, d), F32),
        compiler_params=_params("parallel", "parallel"),
        name="hy_out",
    )(x, ya, yb, wo, mod3)


HALO = 8


def _sc_kernel(x_ref, sh_ref, sc_ref, g_ref, ng_ref, win_ref, cw_ref, wout_ref, o_ref, cx_ref):
    x = x_ref[0]
    tm, d = x.shape

    @pl.when(pl.program_id(1) == 0)
    def _():
        cx_ref[0:HALO, :] = jnp.zeros((HALO, d), F32)

    h = _modulated(x, ng_ref[0], sc_ref[0], sh_ref[0]).astype(BF16)
    b_gate = jnp.dot(h, win_ref[:, 0:d], preferred_element_type=F32)
    c_gate = jnp.dot(h, win_ref[:, d:2 * d], preferred_element_type=F32)
    xs = jnp.dot(h, win_ref[:, 2 * d:3 * d], preferred_element_type=F32)
    cx_ref[HALO:HALO + tm, :] = c_gate * xs
    cw = cw_ref[0]
    y = (cw[2:3] * cx_ref[HALO:HALO + tm, :]
         + cw[1:2] * cx_ref[HALO - 1:HALO - 1 + tm, :]
         + cw[0:1] * cx_ref[HALO - 2:HALO - 2 + tm, :])
    cx_ref[0:HALO, :] = cx_ref[tm:tm + HALO, :]
    m = jnp.dot((b_gate * y).astype(BF16), wout_ref[...], preferred_element_type=F32)
    o_ref[0] = x + g_ref[0] * m


def _short_conv(x, mod3, ng3, win, conv_w, wout, *, layer, sub, idx):
    bsz, s, d = x.shape
    tm = TOKEN_TILE
    return pl.pallas_call(
        _sc_kernel,
        grid=(bsz, s // tm),
        in_specs=[
            pl.BlockSpec((1, tm, d), lambda b, i: (b, i, 0)),
            _mod_spec(layer, 3 * sub + 0, d),
            _mod_spec(layer, 3 * sub + 1, d),
            _mod_spec(layer, 3 * sub + 2, d),
            pl.BlockSpec((1, 1, d), lambda b, i: (layer * 3 + sub, 0, 0)),
            _resident(win.shape, lambda b, i: (0, 0)),
            pl.BlockSpec((1,) + conv_w.shape[1:], lambda b, i: (idx, 0, 0)),
            _resident(wout.shape, lambda b, i: (0, 0)),
        ],
        out_specs=pl.BlockSpec((1, tm, d), lambda b, i: (b, i, 0)),
        out_shape=jax.ShapeDtypeStruct((bsz, s, d), F32),
        scratch_shapes=[pltpu.VMEM((tm + HALO, d), F32)],
        compiler_params=_params("parallel", "arbitrary"),
        name="short_conv",
    )(x, mod3, mod3, mod3, ng3, win, conv_w, wout)


def kernel(x, c, mod_w, mod_b, norm_g, ffn_w_gate, ffn_w_up, ffn_w_down, hy_w_in, hy_w_out,
           gm_vnorm_g, gm_w_s, gm_b_s, sc_w_in, sc_conv_w, sc_w_out, final_norm_g):
    depth = mod_w.shape[0]
    d = x.shape[-1]
    n_sub = norm_g.shape[1]

    mod = _modulation(c, mod_w, mod_b)
    mod3 = mod.reshape(depth * MOD_ROWS * N_MOD, 1, d)
    ng3 = norm_g.reshape(depth * n_sub, 1, d)

    wg, wu, wd = (w.astype(BF16) for w in (ffn_w_gate, ffn_w_up, ffn_w_down))

    for layer in range(depth):
        i = layer // 2
        x = _ffn(x, mod3, ng3, wg, wu, wd, final_norm_g,
                 layer=layer, sub=0, which=0, final=False)
        if layer % 2 == 0:
            n_ha = gm_w_s.shape[1]
            wa_n = 2 * n_ha * HEAD_DIM
            wb = (hy_w_in.shape[2] - wa_n) // 3
            w_in = hy_w_in[i].astype(BF16)
            wa = w_in[:, :wa_n]
            wq = w_in[:, wa_n:wa_n + wb]
            wkt = w_in[:, wa_n + wb:wa_n + 2 * wb].T
            wv = w_in[:, wa_n + 2 * wb:]
            head_of = jnp.arange(wa_n // 2) // HEAD_DIM
            gmat = (head_of[:, None] == head_of[None, :]).astype(BF16)
            ws2 = gm_w_s[i].astype(BF16).reshape(n_ha // 2, 2 * CHUNK, CHUNK)
            bs_tile = jnp.repeat(gm_b_s[i].T, HEAD_DIM, axis=1)
            ya, q, kt, v = _hy_in(x, mod3, ng3, wa, wq, wkt, wv, gm_vnorm_g[i][None, :],
                                  gmat, ws2, bs_tile, layer=layer, sub=1)
            yb = _stick_breaking(q, kt, v)
            x = _hy_out(x, ya, yb, hy_w_out[i].astype(BF16), mod3, layer=layer, sub=1)
        else:
            x = _short_conv(x, mod3, ng3, sc_w_in[i].astype(BF16), sc_conv_w,
                            sc_w_out[i].astype(BF16), layer=layer, sub=1, idx=i)
        x = _ffn(x, mod3, ng3, wg, wu, wd, final_norm_g,
                 layer=layer, sub=2, which=1, final=(layer == depth - 1))
    return x
```

```python
import functools

import jax
import jax.numpy as jnp
from jax import lax
from jax.experimental import pallas as pl
from jax.experimental.pallas import tpu as pltpu

F32 = jnp.float32
BF16 = jnp.bfloat16

HEAD_DIM = 64
CHUNK = 128
EPS = 1e-6
MACARON_WEIGHT = 0.5
N_MOD = 9
MOD_ROWS = 8

LANES = 128
SB_BLOCK = 256
TOKEN_TILE = 512
VMEM_LIMIT = 56 * 1024 * 1024


def _resident(block_shape, index_map):
    return pl.BlockSpec(block_shape, index_map, pipeline_mode=pl.Buffered(1))


def _params(*semantics):
    return pltpu.CompilerParams(dimension_semantics=semantics,
                                vmem_limit_bytes=VMEM_LIMIT)


def _rms(x):
    return lax.rsqrt(jnp.mean(x * x, axis=-1, keepdims=True) + EPS)


def _modulated(x, ng, sc, sh):
    return (x * _rms(x)) * (ng * (1.0 + sc)) + sh


def _silu(x):
    return x * jax.nn.sigmoid(x)


def _mod_kernel(c_ref, w_ref, b_ref, o_ref):
    cond = _silu(c_ref[...]).astype(BF16)
    o_ref[0] = jnp.dot(cond, w_ref[0].astype(BF16),
                       preferred_element_type=F32) + b_ref[0]


def _modulation(c, mod_w, mod_b):
    depth, d, n = mod_w.shape
    bsz = c.shape[0]
    tn = 1024
    c8 = jnp.zeros((MOD_ROWS, d), F32).at[:bsz].set(c)
    return pl.pallas_call(
        _mod_kernel,
        grid=(depth, n // tn),
        in_specs=[
            pl.BlockSpec((MOD_ROWS, d), lambda l, j: (0, 0)),
            pl.BlockSpec((1, d, tn), lambda l, j: (l, 0, j)),
            pl.BlockSpec((1, 1, tn), lambda l, j: (l, 0, j)),
        ],
        out_specs=pl.BlockSpec((1, MOD_ROWS, tn), lambda l, j: (l, 0, j)),
        out_shape=jax.ShapeDtypeStruct((depth, MOD_ROWS, n), F32),
        compiler_params=_params("parallel", "parallel"),
        name="adaln_mod",
    )(c8, mod_w, mod_b.reshape(depth, 1, n))


def _mod_spec(layer, k, d):
    base = layer * MOD_ROWS * N_MOD + k
    return pl.BlockSpec((1, 1, d), lambda b, i: (base + b * N_MOD, 0, 0))


def _ffn_kernel(x_ref, sh_ref, sc_ref, g_ref, ng_ref, wg_ref, wu_ref, wd_ref, fg_ref,
                o_ref, a_ref, *, ff_chunk, final):
    x = x_ref[0]
    h = _modulated(x, ng_ref[0], sc_ref[0], sh_ref[0]).astype(BF16)
    d_ff = a_ref.shape[1]
    for c0 in range(0, d_ff, ff_chunk):
        sl = slice(c0, c0 + ff_chunk)
        g = jnp.dot(h, wg_ref[0, 0, :, sl], preferred_element_type=F32)
        u = jnp.dot(h, wu_ref[0, 0, :, sl], preferred_element_type=F32)
        a_ref[:, sl] = (_silu(g) * u).astype(BF16)
    y = jnp.dot(a_ref[...], wd_ref[0, 0], preferred_element_type=F32)
    out = x + (MACARON_WEIGHT * g_ref[0]) * y
    if final:
        out = (out * _rms(out)) * fg_ref[...]
    o_ref[0] = out


def _ffn(x, mod3, ng3, wg, wu, wd, final_g, *, layer, sub, which, final):
    bsz, s, d = x.shape
    d_ff = wg.shape[-1]
    tm = TOKEN_TILE
    kern = functools.partial(_ffn_kernel, ff_chunk=256, final=final)
    wspec_in = _resident((1, 1, d, d_ff), lambda b, i: (layer, which, 0, 0))
    wspec_out = _resident((1, 1, d_ff, d), lambda b, i: (layer, which, 0, 0))
    return pl.pallas_call(
        kern,
        grid=(bsz, s // tm),
        in_specs=[
            pl.BlockSpec((1, tm, d), lambda b, i: (b, i, 0)),
            _mod_spec(layer, 3 * sub + 0, d),
            _mod_spec(layer, 3 * sub + 1, d),
            _mod_spec(layer, 3 * sub + 2, d),
            pl.BlockSpec((1, 1, d), lambda b, i: (layer * 3 + sub, 0, 0)),
            wspec_in, wspec_in, wspec_out,
            pl.BlockSpec((1, d), lambda b, i: (0, 0)),
        ],
        out_specs=pl.BlockSpec((1, tm, d), lambda b, i: (b, i, 0)),
        out_shape=jax.ShapeDtypeStruct((bsz, s, d), F32),
        scratch_shapes=[pltpu.VMEM((tm, d_ff), BF16)],
        compiler_params=_params("parallel", "parallel"),
        name=f"ffn_l{layer}_{which}",
    )(x, mod3, mod3, mod3, ng3, wg, wu, wd, final_g.reshape(1, d))


def _hyin_kernel(x_ref, sh_ref, sc_ref, ng_ref, wa_ref, wq_ref, wkt_ref, wv_ref, vg_ref,
                 gmat_ref, ws_ref, bs_ref, ya_ref, q_ref, kt_ref, v_ref):
    x = x_ref[0]
    tm = x.shape[0]
    h = _modulated(x, ng_ref[0], sc_ref[0], sh_ref[0]).astype(BF16)

    q = jnp.dot(h, wq_ref[...], preferred_element_type=F32)
    q_ref[0] = (q * (HEAD_DIM ** -0.5)).astype(BF16)
    v_ref[0] = jnp.dot(h, wv_ref[...], preferred_element_type=F32).astype(BF16)
    kt_ref[0] = lax.dot_general(wkt_ref[...], h, (((1,), (1,)), ((), ())),
                                preferred_element_type=F32).astype(BF16)

    uv = jnp.dot(h, wa_ref[...], preferred_element_type=F32)
    uv = 0.5 * uv * (1.0 + lax.erf(uv * (2.0 ** -0.5)))
    wa = uv.shape[1] // 2
    u, v = uv[:, :wa], uv[:, wa:]
    vsq = v * v
    hi = vsq.astype(BF16)
    lo = (vsq - hi.astype(F32)).astype(BF16)
    ssq = (jnp.dot(hi, gmat_ref[...], preferred_element_type=F32)
           + jnp.dot(lo, gmat_ref[...], preferred_element_type=F32))
    vn = (v * lax.rsqrt(ssq * (1.0 / HEAD_DIM) + EPS) * vg_ref[...]).astype(BF16)

    n_pairs = wa // LANES
    r2 = lax.broadcasted_iota(jnp.int32, (2 * CHUNK, CHUNK), 0)
    c2 = lax.broadcasted_iota(jnp.int32, (2 * CHUNK, CHUNK), 1)
    causal = (r2 & (CHUNK - 1)) >= c2
    first_head = lax.broadcasted_iota(jnp.int32, (CHUNK, LANES), 1) < HEAD_DIM
    w_pairs = [jnp.where(causal, ws_ref[p], jnp.zeros_like(ws_ref[p])) for p in range(n_pairs)]
    for ci in range(tm // CHUNK):
        rows = slice(ci * CHUNK, (ci + 1) * CHUNK)
        parts = []
        for p in range(n_pairs):
            vp = vn[rows, p * LANES:(p + 1) * LANES]
            r = jnp.dot(w_pairs[p], vp, preferred_element_type=F32)
            parts.append(jnp.where(first_head, r[:CHUNK], r[CHUNK:]))
        sv = jnp.concatenate(parts, axis=1) + bs_ref[...]
        ya_ref[0, rows, :] = (u[rows] * sv).astype(BF16)


def _hy_in(x, mod3, ng3, wa, wq, wkt, wv, vg, gmat, ws2, bs_tile, *, layer, sub):
    bsz, s, d = x.shape
    tm = TOKEN_TILE
    wa_n, wb = wa.shape[1], wq.shape[1]
    const2 = lambda b, i: (0, 0)
    return pl.pallas_call(
        _hyin_kernel,
        grid=(bsz, s // tm),
        in_specs=[
            pl.BlockSpec((1, tm, d), lambda b, i: (b, i, 0)),
            _mod_spec(layer, 3 * sub + 0, d),
            _mod_spec(layer, 3 * sub + 1, d),
            pl.BlockSpec((1, 1, d), lambda b, i: (layer * 3 + sub, 0, 0)),
            _resident(wa.shape, const2),
            _resident(wq.shape, const2),
            _resident(wkt.shape, const2),
            _resident(wv.shape, const2),
            pl.BlockSpec(vg.shape, const2),
            _resident(gmat.shape, const2),
            _resident(ws2.shape, lambda b, i: (0, 0, 0)),
            pl.BlockSpec(bs_tile.shape, const2),
        ],
        out_specs=[
            pl.BlockSpec((1, tm, wa_n // 2), lambda b, i: (b, i, 0)),
            pl.BlockSpec((1, tm, wb), lambda b, i: (b, i, 0)),
            pl.BlockSpec((1, wb, tm), lambda b, i: (b, 0, i)),
            pl.BlockSpec((1, tm, wb), lambda b, i: (b, i, 0)),
        ],
        out_shape=[
            jax.ShapeDtypeStruct((bsz, s, wa_n // 2), BF16),
            jax.ShapeDtypeStruct((bsz, s, wb), BF16),
            jax.ShapeDtypeStruct((bsz, wb, s), BF16),
            jax.ShapeDtypeStruct((bsz, s, wb), BF16),
        ],
        compiler_params=_params("parallel", "parallel"),
        name="hy_in",
    )(x, mod3, mod3, ng3, wa, wq, wkt, wv, vg, gmat, ws2, bs_tile)


MASKED_LOG = -1e30
LOG2E = 1.4426950408889634


def _sb_kernel(q_ref, kt_ref, v_ref, o_ref,
               acc_ref, c_ref, cb_ref, sp_ref, lb_ref, w_ref):
    t = q_ref.shape[1]
    qi = pl.program_id(2)
    n_blocks = qi + 1
    q = q_ref[0]
    first_head = lax.broadcasted_iota(jnp.int32, (t, LANES), 1) < HEAD_DIM
    zero = jnp.zeros_like(q)
    q_heads = (jnp.where(first_head, q, zero), jnp.where(first_head, zero, q))
    row = lax.broadcasted_iota(jnp.int32, (t, t), 0)
    col = lax.broadcasted_iota(jnp.int32, (t, t), 1)
    later = (row > col).astype(BF16)
    before = col < row

    def scores(k, dst, diag):
        k0 = pl.multiple_of((qi - k) * t, t)
        kt = kt_ref[0, :, pl.ds(k0, t)]
        for hh in range(2):
            z = jnp.dot(q_heads[hh], kt, preferred_element_type=F32)
            e = jnp.exp2(jnp.abs(z) * (-LOG2E))
            sp = jnp.maximum(z, 0.0) + jnp.log(1.0 + e)
            log_b = z - sp
            if diag:
                sp = jnp.where(before, sp, 0.0)
                log_b = jnp.where(before, log_b, MASKED_LOG)
            sp_ref[2 * dst + hh] = sp.astype(BF16)
            lb_ref[2 * dst + hh] = log_b
            cb_ref[2 * dst + hh] = c_ref[hh]
            c_ref[hh] += jnp.sum(sp, axis=1, keepdims=True)

    def weights(src, dst):
        for hh in range(2):
            tail = jnp.dot(sp_ref[2 * src + hh], later, preferred_element_type=F32)
            logw = (lb_ref[2 * src + hh] - tail) - cb_ref[2 * src + hh]
            w_ref[2 * dst + hh] = jnp.exp(logw).astype(BF16)

    def values(k, src):
        valid = jnp.logical_and(k >= 0, k < n_blocks)
        k0 = pl.multiple_of((qi - jnp.clip(k, 0, qi)) * t, t)
        vv = v_ref[0, pl.ds(k0, t), :]
        vv = jnp.where(valid, vv, jnp.zeros_like(vv))
        for hh in range(2):
            acc_ref[hh] += jnp.dot(w_ref[2 * src + hh], vv, preferred_element_type=F32)

    def step(it, new):
        old = 1 - new
        scores(jnp.minimum(it, qi), new, False)
        weights(old, new)
        values(it - 2, old)

    acc_ref[...] = jnp.zeros_like(acc_ref)
    c_ref[...] = jnp.zeros_like(c_ref)
    w_ref[...] = jnp.zeros_like(w_ref)
    scores(0, 0, True)

    def body(p, carry):
        step(2 * p + 1, 1)
        step(2 * p + 2, 0)
        return carry

    lax.fori_loop(0, (n_blocks + 2) // 2, body, 0)
    o_ref[0] = jnp.where(first_head, acc_ref[0], acc_ref[1]).astype(o_ref.dtype)


def _stick_breaking(q, kt, v):
    bsz, s, wb = q.shape
    t = SB_BLOCK
    return pl.pallas_call(
        _sb_kernel,
        grid=(bsz, wb // LANES, s // t),
        in_specs=[
            pl.BlockSpec((1, t, LANES), lambda b, p, i: (b, i, p)),
            pl.BlockSpec((1, LANES, s), lambda b, p, i: (b, p, 0)),
            pl.BlockSpec((1, s, LANES), lambda b, p, i: (b, 0, p)),
        ],
        out_specs=pl.BlockSpec((1, t, LANES), lambda b, p, i: (b, i, p)),
        out_shape=jax.ShapeDtypeStruct((bsz, s, wb), BF16),
        scratch_shapes=[
            pltpu.VMEM((2, t, LANES), F32),
            pltpu.VMEM((2, t, 1), F32),
            pltpu.VMEM((4, t, 1), F32),
            pltpu.VMEM((4, t, t), BF16),
            pltpu.VMEM((4, t, t), F32),
            pltpu.VMEM((4, t, t), BF16),
        ],
        compiler_params=_params("parallel", "parallel", "parallel"),
        name="stick_breaking",
    )(q, kt, v)


def _hyout_kernel(x_ref, ya_ref, yb_ref, wo_ref, g_ref, o_ref):
    wa = ya_ref.shape[2]
    m = (jnp.dot(ya_ref[0], wo_ref[:wa, :], preferred_element_type=F32)
         + jnp.dot(yb_ref[0], wo_ref[wa:, :], preferred_element_type=F32))
    o_ref[0] = x_ref[0] + g_ref[0] * m


def _hy_out(x, ya, yb, wo, mod3, *, layer, sub):
    bsz, s, d = x.shape
    tm = TOKEN_TILE
    return pl.pallas_call(
        _hyout_kernel,
        grid=(bsz, s // tm),
        in_specs=[
            pl.BlockSpec((1, tm, d), lambda b, i: (b, i, 0)),
            pl.BlockSpec((1, tm, ya.shape[2]), lambda b, i: (b, i, 0)),
            pl.BlockSpec((1, tm, yb.shape[2]), lambda b, i: (b, i, 0)),
            _resident(wo.shape, lambda b, i: (0, 0)),
            _mod_spec(layer, 3 * sub + 2, d),
        ],
        out_specs=pl.BlockSpec((1, tm, d), lambda b, i: (b, i, 0)),
        out_shape=jax.ShapeDtypeStruct((bsz, s, d), F32),
        compiler_params=_params("parallel", "parallel"),
        name="hy_out",
    )(x, ya, yb, wo, mod3)


HALO = 8


def _sc_kernel(x_ref, sh_ref, sc_ref, g_ref, ng_ref, win_ref, cw_ref, wout_ref, o_ref, cx_ref):
    x = x_ref[0]
    tm, d = x.shape

    @pl.when(pl.program_id(1) == 0)
    def _():
        cx_ref[0:HALO, :] = jnp.zeros((HALO, d), F32)

    h = _modulated(x, ng_ref[0], sc_ref[0], sh_ref[0]).astype(BF16)
    b_gate = jnp.dot(h, win_ref[:, 0:d], preferred_element_type=F32)
    c_gate = jnp.dot(h, win_ref[:, d:2 * d], preferred_element_type=F32)
    xs = jnp.dot(h, win_ref[:, 2 * d:3 * d], preferred_element_type=F32)
    cx_ref[HALO:HALO + tm, :] = c_gate * xs
    cw = cw_ref[0]
    y = (cw[2:3] * cx_ref[HALO:HALO + tm, :]
         + cw[1:2] * cx_ref[HALO - 1:HALO - 1 + tm, :]
         + cw[0:1] * cx_ref[HALO - 2:HALO - 2 + tm, :])
    cx_ref[0:HALO, :] = cx_ref[tm:tm + HALO, :]
    m = jnp.dot((b_gate * y).astype(BF16), wout_ref[...], preferred_element_type=F32)
    o_ref[0] = x + g_ref[0] * m


def _short_conv(x, mod3, ng3, win, conv_w, wout, *, layer, sub, idx):
    bsz, s, d = x.shape
    tm = TOKEN_TILE
    return pl.pallas_call(
        _sc_kernel,
        grid=(bsz, s // tm),
        in_specs=[
            pl.BlockSpec((1, tm, d), lambda b, i: (b, i, 0)),
            _mod_spec(layer, 3 * sub + 0, d),
            _mod_spec(layer, 3 * sub + 1, d),
            _mod_spec(layer, 3 * sub + 2, d),
            pl.BlockSpec((1, 1, d), lambda b, i: (layer * 3 + sub, 0, 0)),
            _resident(win.shape, lambda b, i: (0, 0)),
            pl.BlockSpec((1,) + conv_w.shape[1:], lambda b, i: (idx, 0, 0)),
            _resident(wout.shape, lambda b, i: (0, 0)),
        ],
        out_specs=pl.BlockSpec((1, tm, d), lambda b, i: (b, i, 0)),
        out_shape=jax.ShapeDtypeStruct((bsz, s, d), F32),
        scratch_shapes=[pltpu.VMEM((tm + HALO, d), F32)],
        compiler_params=_params("parallel", "arbitrary"),
        name="short_conv",
    )(x, mod3, mod3, mod3, ng3, win, conv_w, wout)


def kernel(x, c, mod_w, mod_b, norm_g, ffn_w_gate, ffn_w_up, ffn_w_down, hy_w_in, hy_w_out,
           gm_vnorm_g, gm_w_s, gm_b_s, sc_w_in, sc_conv_w, sc_w_out, final_norm_g):
    depth = mod_w.shape[0]
    d = x.shape[-1]
    n_sub = norm_g.shape[1]

    mod = _modulation(c, mod_w, mod_b)
    mod3 = mod.reshape(depth * MOD_ROWS * N_MOD, 1, d)
    ng3 = norm_g.reshape(depth * n_sub, 1, d)

    wg, wu, wd = (w.astype(BF16) for w in (ffn_w_gate, ffn_w_up, ffn_w_down))

    for layer in range(depth):
        i = layer // 2
        x = _ffn(x, mod3, ng3, wg, wu, wd, final_norm_g,
                 layer=layer, sub=0, which=0, final=False)
        if layer % 2 == 0:
            n_ha = gm_w_s.shape[1]
            wa_n = 2 * n_ha * HEAD_DIM
            wb = (hy_w_in.shape[2] - wa_n) // 3
            w_in = hy_w_in[i].astype(BF16)
            wa = w_in[:, :wa_n]
            wq = w_in[:, wa_n:wa_n + wb]
            wkt = w_in[:, wa_n + wb:wa_n + 2 * wb].T
            wv = w_in[:, wa_n + 2 * wb:]
            head_of = jnp.arange(wa_n // 2) // HEAD_DIM
            gmat = (head_of[:, None] == head_of[None, :]).astype(BF16)
            ws2 = gm_w_s[i].astype(BF16).reshape(n_ha // 2, 2 * CHUNK, CHUNK)
            bs_tile = jnp.repeat(gm_b_s[i].T, HEAD_DIM, axis=1)
            ya, q, kt, v = _hy_in(x, mod3, ng3, wa, wq, wkt, wv, gm_vnorm_g[i][None, :],
                                  gmat, ws2, bs_tile, layer=layer, sub=1)
            yb = _stick_breaking(q, kt, v)
            x = _hy_out(x, ya, yb, hy_w_out[i].astype(BF16), mod3, layer=layer, sub=1)
        else:
            x = _short_conv(x, mod3, ng3, sc_w_in[i].astype(BF16), sc_conv_w,
                            sc_w_out[i].astype(BF16), layer=layer, sub=1, idx=i)
        x = _ffn(x, mod3, ng3, wg, wu, wd, final_norm_g,
                 layer=layer, sub=2, which=1, final=(layer == depth - 1))
    return x
```

```python
import functools

import jax
import jax.numpy as jnp
from jax import lax
from jax.experimental import pallas as pl
from jax.experimental.pallas import tpu as pltpu

F32 = jnp.float32
BF16 = jnp.bfloat16

HEAD_DIM = 64
CHUNK = 128
EPS = 1e-6
MACARON_WEIGHT = 0.5
N_MOD = 9
MOD_ROWS = 8

LANES = 128
SB_BLOCK = 256
TOKEN_TILE = 512
VMEM_LIMIT = 56 * 1024 * 1024


def _resident(block_shape, index_map):
    return pl.BlockSpec(block_shape, index_map, pipeline_mode=pl.Buffered(1))


def _params(*semantics):
    return pltpu.CompilerParams(dimension_semantics=semantics,
                                vmem_limit_bytes=VMEM_LIMIT)


def _rms(x):
    return lax.rsqrt(jnp.mean(x * x, axis=-1, keepdims=True) + EPS)


def _modulated(x, ng, sc, sh):
    return (x * _rms(x)) * (ng * (1.0 + sc)) + sh


def _silu(x):
    return x * jax.nn.sigmoid(x)


def _mod_kernel(c_ref, w_ref, b_ref, o_ref):
    cond = _silu(c_ref[...]).astype(BF16)
    o_ref[0] = jnp.dot(cond, w_ref[0].astype(BF16),
                       preferred_element_type=F32) + b_ref[0]


def _modulation(c, mod_w, mod_b):
    depth, d, n = mod_w.shape
    bsz = c.shape[0]
    tn = 1024
    c8 = jnp.zeros((MOD_ROWS, d), F32).at[:bsz].set(c)
    return pl.pallas_call(
        _mod_kernel,
        grid=(depth, n // tn),
        in_specs=[
            pl.BlockSpec((MOD_ROWS, d), lambda l, j: (0, 0)),
            pl.BlockSpec((1, d, tn), lambda l, j: (l, 0, j)),
            pl.BlockSpec((1, 1, tn), lambda l, j: (l, 0, j)),
        ],
        out_specs=pl.BlockSpec((1, MOD_ROWS, tn), lambda l, j: (l, 0, j)),
        out_shape=jax.ShapeDtypeStruct((depth, MOD_ROWS, n), F32),
        compiler_params=_params("parallel", "parallel"),
        name="adaln_mod",
    )(c8, mod_w, mod_b.reshape(depth, 1, n))


def _mod_spec(layer, k, d):
    base = layer * MOD_ROWS * N_MOD + k
    return pl.BlockSpec((1, 1, d), lambda b, i: (base + b * N_MOD, 0, 0))


def _ffn_kernel(x_ref, sh_ref, sc_ref, g_ref, ng_ref, wg_ref, wu_ref, wd_ref, fg_ref,
                o_ref, a_ref, *, ff_chunk, final):
    x = x_ref[0]
    h = _modulated(x, ng_ref[0], sc_ref[0], sh_ref[0]).astype(BF16)
    d_ff = a_ref.shape[1]
    for c0 in range(0, d_ff, ff_chunk):
        sl = slice(c0, c0 + ff_chunk)
        g = jnp.dot(h, wg_ref[0, 0, :, sl], preferred_element_type=F32)
        u = jnp.dot(h, wu_ref[0, 0, :, sl], preferred_element_type=F32)
        a_ref[:, sl] = (_silu(g) * u).astype(BF16)
    y = jnp.dot(a_ref[...], wd_ref[0, 0], preferred_element_type=F32)
    out = x + (MACARON_WEIGHT * g_ref[0]) * y
    if final:
        out = (out * _rms(out)) * fg_ref[...]
    o_ref[0] = out


def _ffn(x, mod3, ng3, wg, wu, wd, final_g, *, layer, sub, which, final):
    bsz, s, d = x.shape
    d_ff = wg.shape[-1]
    tm = TOKEN_TILE
    kern = functools.partial(_ffn_kernel, ff_chunk=256, final=final)
    wspec_in = _resident((1, 1, d, d_ff), lambda b, i: (layer, which, 0, 0))
    wspec_out = _resident((1, 1, d_ff, d), lambda b, i: (layer, which, 0, 0))
    return pl.pallas_call(
        kern,
        grid=(bsz, s // tm),
        in_specs=[
            pl.BlockSpec((1, tm, d), lambda b, i: (b, i, 0)),
            _mod_spec(layer, 3 * sub + 0, d),
            _mod_spec(layer, 3 * sub + 1, d),
            _mod_spec(layer, 3 * sub + 2, d),
            pl.BlockSpec((1, 1, d), lambda b, i: (layer * 3 + sub, 0, 0)),
            wspec_in, wspec_in, wspec_out,
            pl.BlockSpec((1, d), lambda b, i: (0, 0)),
        ],
        out_specs=pl.BlockSpec((1, tm, d), lambda b, i: (b, i, 0)),
        out_shape=jax.ShapeDtypeStruct((bsz, s, d), F32),
        scratch_shapes=[pltpu.VMEM((tm, d_ff), BF16)],
        compiler_params=_params("parallel", "parallel"),
        name=f"ffn_l{layer}_{which}",
    )(x, mod3, mod3, mod3, ng3, wg, wu, wd, final_g.reshape(1, d))


def _hyin_kernel(x_ref, sh_ref, sc_ref, ng_ref, wa_ref, wq_ref, wkt_ref, wv_ref, vg_ref,
                 gmat_ref, ws_ref, bs_ref, ya_ref, q_ref, kt_ref, v_ref):
    x = x_ref[0]
    tm = x.shape[0]
    h = _modulated(x, ng_ref[0], sc_ref[0], sh_ref[0]).astype(BF16)

    q = jnp.dot(h, wq_ref[...], preferred_element_type=F32)
    q_ref[0] = (q * (HEAD_DIM ** -0.5)).astype(BF16)
    v_ref[0] = jnp.dot(h, wv_ref[...], preferred_element_type=F32).astype(BF16)
    kt_ref[0] = lax.dot_general(wkt_ref[...], h, (((1,), (1,)), ((), ())),
                                preferred_element_type=F32).astype(BF16)

    uv = jnp.dot(h, wa_ref[...], preferred_element_type=F32)
    uv = 0.5 * uv * (1.0 + lax.erf(uv * (2.0 ** -0.5)))
    wa = uv.shape[1] // 2
    u, v = uv[:, :wa], uv[:, wa:]
    vsq = v * v
    hi = vsq.astype(BF16)
    lo = (vsq - hi.astype(F32)).astype(BF16)
    ssq = (jnp.dot(hi, gmat_ref[...], preferred_element_type=F32)
           + jnp.dot(lo, gmat_ref[...], preferred_element_type=F32))
    vn = (v * lax.rsqrt(ssq * (1.0 / HEAD_DIM) + EPS) * vg_ref[...]).astype(BF16)

    n_pairs = wa // LANES
    r2 = lax.broadcasted_iota(jnp.int32, (2 * CHUNK, CHUNK), 0)
    c2 = lax.broadcasted_iota(jnp.int32, (2 * CHUNK, CHUNK), 1)
    causal = (r2 & (CHUNK - 1)) >= c2
    first_head = lax.broadcasted_iota(jnp.int32, (CHUNK, LANES), 1) < HEAD_DIM
    w_pairs = [jnp.where(causal, ws_ref[p], jnp.zeros_like(ws_ref[p])) for p in range(n_pairs)]
    for ci in range(tm // CHUNK):
        rows = slice(ci * CHUNK, (ci + 1) * CHUNK)
        parts = []
        for p in range(n_pairs):
            vp = vn[rows, p * LANES:(p + 1) * LANES]
            r = jnp.dot(w_pairs[p], vp, preferred_element_type=F32)
            parts.append(jnp.where(first_head, r[:CHUNK], r[CHUNK:]))
        sv = jnp.concatenate(parts, axis=1) + bs_ref[...]
        ya_ref[0, rows, :] = (u[rows] * sv).astype(BF16)


def _hy_in(x, mod3, ng3, wa, wq, wkt, wv, vg, gmat, ws2, bs_tile, *, layer, sub):
    bsz, s, d = x.shape
    tm = TOKEN_TILE
    wa_n, wb = wa.shape[1], wq.shape[1]
    const2 = lambda b, i: (0, 0)
    return pl.pallas_call(
        _hyin_kernel,
        grid=(bsz, s // tm),
        in_specs=[
            pl.BlockSpec((1, tm, d), lambda b, i: (b, i, 0)),
            _mod_spec(layer, 3 * sub + 0, d),
            _mod_spec(layer, 3 * sub + 1, d),
            pl.BlockSpec((1, 1, d), lambda b, i: (layer * 3 + sub, 0, 0)),
            _resident(wa.shape, const2),
            _resident(wq.shape, const2),
            _resident(wkt.shape, const2),
            _resident(wv.shape, const2),
            pl.BlockSpec(vg.shape, const2),
            _resident(gmat.shape, const2),
            _resident(ws2.shape, lambda b, i: (0, 0, 0)),
            pl.BlockSpec(bs_tile.shape, const2),
        ],
        out_specs=[
            pl.BlockSpec((1, tm, wa_n // 2), lambda b, i: (b, i, 0)),
            pl.BlockSpec((1, tm, wb), lambda b, i: (b, i, 0)),
            pl.BlockSpec((1, wb, tm), lambda b, i: (b, 0, i)),
            pl.BlockSpec((1, tm, wb), lambda b, i: (b, i, 0)),
        ],
        out_shape=[
            jax.ShapeDtypeStruct((bsz, s, wa_n // 2), BF16),
            jax.ShapeDtypeStruct((bsz, s, wb), BF16),
            jax.ShapeDtypeStruct((bsz, wb, s), BF16),
            jax.ShapeDtypeStruct((bsz, s, wb), BF16),
        ],
        compiler_params=_params("parallel", "parallel"),
        name="hy_in",
    )(x, mod3, mod3, ng3, wa, wq, wkt, wv, vg, gmat, ws2, bs_tile)


MASKED_LOG = -1e30
LOG2E = 1.4426950408889634
SATURATED = 105.0


def _sb_kernel(q_ref, kt_ref, v_ref, o_ref,
               acc_ref, c_ref, cb_ref, sp_ref, lb_ref, w_ref):
    t = q_ref.shape[1]
    qi = pl.program_id(2)
    q = q_ref[0]
    first_head = lax.broadcasted_iota(jnp.int32, (t, LANES), 1) < HEAD_DIM
    zero = jnp.zeros_like(q)
    q_heads = (jnp.where(first_head, q, zero), jnp.where(first_head, zero, q))
    row = lax.broadcasted_iota(jnp.int32, (t, t), 0)
    col = lax.broadcasted_iota(jnp.int32, (t, t), 1)
    later = (row > col).astype(BF16)
    before = col < row

    def scores(k, dst, diag):
        k0 = pl.multiple_of((qi - k) * t, t)
        kt = kt_ref[0, :, pl.ds(k0, t)]
        for hh in range(2):
            z = jnp.dot(q_heads[hh], kt, preferred_element_type=F32)
            e = jnp.exp2(jnp.abs(z) * (-LOG2E))
            sp = jnp.maximum(z, 0.0) + jnp.log(1.0 + e)
            log_b = z - sp
            if diag:
                sp = jnp.where(before, sp, 0.0)
                log_b = jnp.where(before, log_b, MASKED_LOG)
            sp_ref[2 * dst + hh] = sp.astype(BF16)
            lb_ref[2 * dst + hh] = log_b
            cb_ref[2 * dst + hh] = c_ref[hh]
            c_ref[hh] += jnp.sum(sp, axis=1, keepdims=True)

    def weights(src, dst):
        for hh in range(2):
            tail = jnp.dot(sp_ref[2 * src + hh], later, preferred_element_type=F32)
            logw = (lb_ref[2 * src + hh] - tail) - cb_ref[2 * src + hh]
            w_ref[2 * dst + hh] = jnp.exp(logw).astype(BF16)

    def values(k, src, valid=None):
        k0 = pl.multiple_of((qi - k) * t, t)
        vv = v_ref[0, pl.ds(k0, t), :]
        if valid is not None:
            vv = jnp.where(valid, vv, jnp.zeros_like(vv))
        for hh in range(2):
            acc_ref[hh] += jnp.dot(w_ref[2 * src + hh], vv, preferred_element_type=F32)

    def saturated():
        return jnp.min(c_ref[...]) >= SATURATED

    acc_ref[...] = jnp.zeros_like(acc_ref)
    c_ref[...] = jnp.zeros_like(c_ref)

    second = jnp.minimum(qi, 1)
    scores(0, 0, True)
    scores(second, 1, False)
    weights(0, 0)
    values(0, 0)
    weights(1, 1)
    values(second, 1, qi >= 1)

    rest = qi - 1

    def step(it, new):
        old = 1 - new
        scores(jnp.minimum(2 + it, qi), new, False)
        weights(old, new)
        r = it - 2
        values(jnp.clip(2 + r, 2, qi), old, jnp.logical_and(r >= 0, r < rest))

    @pl.when(jnp.logical_and(rest > 0, jnp.logical_not(saturated())))
    def _():
        scores(2, 0, False)

        def cond(state):
            p, stop = state
            return jnp.logical_and(p < (rest + 2) // 2, stop == 0)

        def body(state):
            p, _ = state
            stop = saturated().astype(jnp.int32)
            step(2 * p + 1, 1)
            step(2 * p + 2, 0)
            return p + 1, stop

        lax.while_loop(cond, body, (jnp.int32(0), jnp.int32(0)))

    o_ref[0] = jnp.where(first_head, acc_ref[0], acc_ref[1]).astype(o_ref.dtype)


def _stick_breaking(q, kt, v):
    bsz, s, wb = q.shape
    t = SB_BLOCK
    return pl.pallas_call(
        _sb_kernel,
        grid=(bsz, wb // LANES, s // t),
        in_specs=[
            pl.BlockSpec((1, t, LANES), lambda b, p, i: (b, i, p)),
            pl.BlockSpec((1, LANES, s), lambda b, p, i: (b, p, 0)),
            pl.BlockSpec((1, s, LANES), lambda b, p, i: (b, 0, p)),
        ],
        out_specs=pl.BlockSpec((1, t, LANES), lambda b, p, i: (b, i, p)),
        out_shape=jax.ShapeDtypeStruct((bsz, s, wb), BF16),
        scratch_shapes=[
            pltpu.VMEM((2, t, LANES), F32),
            pltpu.VMEM((2, t, 1), F32),
            pltpu.VMEM((4, t, 1), F32),
            pltpu.VMEM((4, t, t), BF16),
            pltpu.VMEM((4, t, t), F32),
            pltpu.VMEM((4, t, t), BF16),
        ],
        compiler_params=_params("parallel", "parallel", "parallel"),
        name="stick_breaking",
    )(q, kt, v)


def _hyout_kernel(x_ref, ya_ref, yb_ref, wo_ref, g_ref, o_ref):
    wa = ya_ref.shape[2]
    m = (jnp.dot(ya_ref[0], wo_ref[:wa, :], preferred_element_type=F32)
         + jnp.dot(yb_ref[0], wo_ref[wa:, :], preferred_element_type=F32))
    o_ref[0] = x_ref[0] + g_ref[0] * m


def _hy_out(x, ya, yb, wo, mod3, *, layer, sub):
    bsz, s, d = x.shape
    tm = TOKEN_TILE
    return pl.pallas_call(
        _hyout_kernel,
        grid=(bsz, s // tm),
        in_specs=[
            pl.BlockSpec((1, tm, d), lambda b, i: (b, i, 0)),
            pl.BlockSpec((1, tm, ya.shape[2]), lambda b, i: (b, i, 0)),
            pl.BlockSpec((1, tm, yb.shape[2]), lambda b, i: (b, i, 0)),
            _resident(wo.shape, lambda b, i: (0, 0)),
            _mod_spec(layer, 3 * sub + 2, d),
        ],
        out_specs=pl.BlockSpec((1, tm, d), lambda b, i: (b, i, 0)),
        out_shape=jax.ShapeDtypeStruct((bsz, s, d), F32),
        compiler_params=_params("parallel", "parallel"),
        name="hy_out",
    )(x, ya, yb, wo, mod3)


HALO = 8


def _sc_kernel(x_ref, sh_ref, sc_ref, g_ref, ng_ref, win_ref, cw_ref, wout_ref, o_ref, cx_ref):
    x = x_ref[0]
    tm, d = x.shape

    @pl.when(pl.program_id(1) == 0)
    def _():
        cx_ref[0:HALO, :] = jnp.zeros((HALO, d), F32)

    h = _modulated(x, ng_ref[0], sc_ref[0], sh_ref[0]).astype(BF16)
    b_gate = jnp.dot(h, win_ref[:, 0:d], preferred_element_type=F32)
    c_gate = jnp.dot(h, win_ref[:, d:2 * d], preferred_element_type=F32)
    xs = jnp.dot(h, win_ref[:, 2 * d:3 * d], preferred_element_type=F32)
    cx_ref[HALO:HALO + tm, :] = c_gate * xs
    cw = cw_ref[0]
    y = (cw[2:3] * cx_ref[HALO:HALO + tm, :]
         + cw[1:2] * cx_ref[HALO - 1:HALO - 1 + tm, :]
         + cw[0:1] * cx_ref[HALO - 2:HALO - 2 + tm, :])
    cx_ref[0:HALO, :] = cx_ref[tm:tm + HALO, :]
    m = jnp.dot((b_gate * y).astype(BF16), wout_ref[...], preferred_element_type=F32)
    o_ref[0] = x + g_ref[0] * m


def _short_conv(x, mod3, ng3, win, conv_w, wout, *, layer, sub, idx):
    bsz, s, d = x.shape
    tm = TOKEN_TILE
    return pl.pallas_call(
        _sc_kernel,
        grid=(bsz, s // tm),
        in_specs=[
            pl.BlockSpec((1, tm, d), lambda b, i: (b, i, 0)),
            _mod_spec(layer, 3 * sub + 0, d),
            _mod_spec(layer, 3 * sub + 1, d),
            _mod_spec(layer, 3 * sub + 2, d),
            pl.BlockSpec((1, 1, d), lambda b, i: (layer * 3 + sub, 0, 0)),
            _resident(win.shape, lambda b, i: (0, 0)),
            pl.BlockSpec((1,) + conv_w.shape[1:], lambda b, i: (idx, 0, 0)),
            _resident(wout.shape, lambda b, i: (0, 0)),
        ],
        out_specs=pl.BlockSpec((1, tm, d), lambda b, i: (b, i, 0)),
        out_shape=jax.ShapeDtypeStruct((bsz, s, d), F32),
        scratch_shapes=[pltpu.VMEM((tm + HALO, d), F32)],
        compiler_params=_params("parallel", "arbitrary"),
        name="short_conv",
    )(x, mod3, mod3, mod3, ng3, win, conv_w, wout)


def kernel(x, c, mod_w, mod_b, norm_g, ffn_w_gate, ffn_w_up, ffn_w_down, hy_w_in, hy_w_out,
           gm_vnorm_g, gm_w_s, gm_b_s, sc_w_in, sc_conv_w, sc_w_out, final_norm_g):
    depth = mod_w.shape[0]
    d = x.shape[-1]
    n_sub = norm_g.shape[1]

    mod = _modulation(c, mod_w, mod_b)
    mod3 = mod.reshape(depth * MOD_ROWS * N_MOD, 1, d)
    ng3 = norm_g.reshape(depth * n_sub, 1, d)

    wg, wu, wd = (w.astype(BF16) for w in (ffn_w_gate, ffn_w_up, ffn_w_down))

    for layer in range(depth):
        i = layer // 2
        x = _ffn(x, mod3, ng3, wg, wu, wd, final_norm_g,
                 layer=layer, sub=0, which=0, final=False)
        if layer % 2 == 0:
            n_ha = gm_w_s.shape[1]
            wa_n = 2 * n_ha * HEAD_DIM
            wb = (hy_w_in.shape[2] - wa_n) // 3
            w_in = hy_w_in[i].astype(BF16)
            wa = w_in[:, :wa_n]
            wq = w_in[:, wa_n:wa_n + wb]
            wkt = w_in[:, wa_n + wb:wa_n + 2 * wb].T
            wv = w_in[:, wa_n + 2 * wb:]
            head_of = jnp.arange(wa_n // 2) // HEAD_DIM
            gmat = (head_of[:, None] == head_of[None, :]).astype(BF16)
            ws2 = gm_w_s[i].astype(BF16).reshape(n_ha // 2, 2 * CHUNK, CHUNK)
            bs_tile = jnp.repeat(gm_b_s[i].T, HEAD_DIM, axis=1)
            ya, q, kt, v = _hy_in(x, mod3, ng3, wa, wq, wkt, wv, gm_vnorm_g[i][None, :],
                                  gmat, ws2, bs_tile, layer=layer, sub=1)
            yb = _stick_breaking(q, kt, v)
            x = _hy_out(x, ya, yb, hy_w_out[i].astype(BF16), mod3, layer=layer, sub=1)
        else:
            x = _short_conv(x, mod3, ng3, sc_w_in[i].astype(BF16), sc_conv_w,
                            sc_w_out[i].astype(BF16), layer=layer, sub=1, idx=i)
        x = _ffn(x, mod3, ng3, wg, wu, wd, final_norm_g,
                 layer=layer, sub=2, which=1, final=(layer == depth - 1))
    return x
```

```python
import functools

import jax
import jax.numpy as jnp
from jax import lax
from jax.experimental import pallas as pl
from jax.experimental.pallas import tpu as pltpu

F32 = jnp.float32
BF16 = jnp.bfloat16

HEAD_DIM = 64
CHUNK = 128
EPS = 1e-6
MACARON_WEIGHT = 0.5
N_MOD = 9
MOD_ROWS = 8

LANES = 128
SB_BLOCK = 256
TOKEN_TILE = 1024
VMEM_LIMIT = 56 * 1024 * 1024


def _resident(block_shape, index_map):
    return pl.BlockSpec(block_shape, index_map, pipeline_mode=pl.Buffered(1))


def _params(*semantics):
    return pltpu.CompilerParams(dimension_semantics=semantics,
                                vmem_limit_bytes=VMEM_LIMIT)


def _rms(x):
    return lax.rsqrt(jnp.mean(x * x, axis=-1, keepdims=True) + EPS)


def _modulated(x, ng, sc, sh):
    return (x * _rms(x)) * (ng * (1.0 + sc)) + sh


def _silu(x):
    return x * jax.nn.sigmoid(x)


def _mod_kernel(c_ref, w_ref, b_ref, o_ref):
    cond = _silu(c_ref[...]).astype(BF16)
    o_ref[0] = jnp.dot(cond, w_ref[0].astype(BF16),
                       preferred_element_type=F32) + b_ref[0]


def _modulation(c, mod_w, mod_b):
    depth, d, n = mod_w.shape
    bsz = c.shape[0]
    tn = 1024
    c8 = jnp.zeros((MOD_ROWS, d), F32).at[:bsz].set(c)
    return pl.pallas_call(
        _mod_kernel,
        grid=(depth, n // tn),
        in_specs=[
            pl.BlockSpec((MOD_ROWS, d), lambda l, j: (0, 0)),
            pl.BlockSpec((1, d, tn), lambda l, j: (l, 0, j)),
            pl.BlockSpec((1, 1, tn), lambda l, j: (l, 0, j)),
        ],
        out_specs=pl.BlockSpec((1, MOD_ROWS, tn), lambda l, j: (l, 0, j)),
        out_shape=jax.ShapeDtypeStruct((depth, MOD_ROWS, n), F32),
        compiler_params=_params("parallel", "parallel"),
        name="adaln_mod",
    )(c8, mod_w, mod_b.reshape(depth, 1, n))


def _mod_spec(layer, k, d):
    base = layer * MOD_ROWS * N_MOD + k
    return pl.BlockSpec((1, 1, d), lambda b, i: (base + b * N_MOD, 0, 0))


def _ffn_kernel(*refs, ff_chunk, final, mixed):
    if mixed:
        x_ref, ya_ref, yb_ref, wo_ref, gm_ref = refs[:5]
        wa = ya_ref.shape[2]
        m = (jnp.dot(ya_ref[0], wo_ref[:wa, :], preferred_element_type=F32)
             + jnp.dot(yb_ref[0], wo_ref[wa:, :], preferred_element_type=F32))
        x = x_ref[0] + gm_ref[0] * m
        refs = refs[5:]
    else:
        x = refs[0][0]
        refs = refs[1:]
    sh_ref, sc_ref, g_ref, ng_ref, wg_ref, wu_ref, wd_ref, fg_ref, o_ref, a_ref = refs
    h = _modulated(x, ng_ref[0], sc_ref[0], sh_ref[0]).astype(BF16)
    d_ff = a_ref.shape[1]
    for c0 in range(0, d_ff, ff_chunk):
        sl = slice(c0, c0 + ff_chunk)
        g = jnp.dot(h, wg_ref[0, 0, :, sl], preferred_element_type=F32)
        u = jnp.dot(h, wu_ref[0, 0, :, sl], preferred_element_type=F32)
        a_ref[:, sl] = (_silu(g) * u).astype(BF16)
    y = jnp.dot(a_ref[...], wd_ref[0, 0], preferred_element_type=F32)
    out = x + (MACARON_WEIGHT * g_ref[0]) * y
    if final:
        out = (out * _rms(out)) * fg_ref[...]
    o_ref[0] = out


def _ffn(x, mod3, ng3, wg, wu, wd, final_g, *, layer, sub, which, final, mix=None):
    bsz, s, d = x.shape
    d_ff = wg.shape[-1]
    tm = TOKEN_TILE
    kern = functools.partial(_ffn_kernel, ff_chunk=256, final=final, mixed=mix is not None)
    wspec_in = _resident((1, 1, d, d_ff), lambda b, i: (layer, which, 0, 0))
    wspec_out = _resident((1, 1, d_ff, d), lambda b, i: (layer, which, 0, 0))
    mix_specs, mix_args = [], []
    if mix is not None:
        ya, yb, wo = mix
        mix_specs = [
            pl.BlockSpec((1, tm, ya.shape[2]), lambda b, i: (b, i, 0)),
            pl.BlockSpec((1, tm, yb.shape[2]), lambda b, i: (b, i, 0)),
            _resident(wo.shape, lambda b, i: (0, 0)),
            _mod_spec(layer, 3 * (sub - 1) + 2, d),
        ]
        mix_args = [ya, yb, wo, mod3]
    return pl.pallas_call(
        kern,
        grid=(bsz, s // tm),
        in_specs=[
            pl.BlockSpec((1, tm, d), lambda b, i: (b, i, 0)),
            *mix_specs,
            _mod_spec(layer, 3 * sub + 0, d),
            _mod_spec(layer, 3 * sub + 1, d),
            _mod_spec(layer, 3 * sub + 2, d),
            pl.BlockSpec((1, 1, d), lambda b, i: (layer * 3 + sub, 0, 0)),
            wspec_in, wspec_in, wspec_out,
            pl.BlockSpec((1, d), lambda b, i: (0, 0)),
        ],
        out_specs=pl.BlockSpec((1, tm, d), lambda b, i: (b, i, 0)),
        out_shape=jax.ShapeDtypeStruct((bsz, s, d), F32),
        scratch_shapes=[pltpu.VMEM((tm, d_ff), BF16)],
        compiler_params=_params("parallel", "parallel"),
        name=f"ffn_l{layer}_{which}",
    )(x, *mix_args, mod3, mod3, mod3, ng3, wg, wu, wd, final_g.reshape(1, d))


def _hyin_kernel(x_ref, sh_ref, sc_ref, ng_ref, wa_ref, wq_ref, wkt_ref, wv_ref, vg_ref,
                 gmat_ref, ws_ref, bs_ref, ya_ref, q_ref, kt_ref, v_ref):
    x = x_ref[0]
    tm = x.shape[0]
    h = _modulated(x, ng_ref[0], sc_ref[0], sh_ref[0]).astype(BF16)

    q = jnp.dot(h, wq_ref[...], preferred_element_type=F32)
    q_ref[0] = (q * (HEAD_DIM ** -0.5)).astype(BF16)
    v_ref[0] = jnp.dot(h, wv_ref[...], preferred_element_type=F32).astype(BF16)
    kt_ref[0] = lax.dot_general(wkt_ref[...], h, (((1,), (1,)), ((), ())),
                                preferred_element_type=F32).astype(BF16)

    uv = jnp.dot(h, wa_ref[...], preferred_element_type=F32)
    uv = 0.5 * uv * (1.0 + lax.erf(uv * (2.0 ** -0.5)))
    wa = uv.shape[1] // 2
    u, v = uv[:, :wa], uv[:, wa:]
    ssq = jnp.dot((v * v).astype(BF16), gmat_ref[...], preferred_element_type=F32)
    vn = (v * lax.rsqrt(ssq * (1.0 / HEAD_DIM) + EPS) * vg_ref[...]).astype(BF16)

    n_pairs = wa // LANES
    r2 = lax.broadcasted_iota(jnp.int32, (2 * CHUNK, CHUNK), 0)
    c2 = lax.broadcasted_iota(jnp.int32, (2 * CHUNK, CHUNK), 1)
    causal = (r2 & (CHUNK - 1)) >= c2
    first_head = lax.broadcasted_iota(jnp.int32, (CHUNK, LANES), 1) < HEAD_DIM
    w_pairs = [jnp.where(causal, ws_ref[p], jnp.zeros_like(ws_ref[p])) for p in range(n_pairs)]
    for ci in range(tm // CHUNK):
        rows = slice(ci * CHUNK, (ci + 1) * CHUNK)
        parts = []
        for p in range(n_pairs):
            vp = vn[rows, p * LANES:(p + 1) * LANES]
            r = jnp.dot(w_pairs[p], vp, preferred_element_type=F32)
            parts.append(jnp.where(first_head, r[:CHUNK], r[CHUNK:]))
        sv = jnp.concatenate(parts, axis=1) + bs_ref[...]
        ya_ref[0, rows, :] = (u[rows] * sv).astype(BF16)


def _hy_in(x, mod3, ng3, wa, wq, wkt, wv, vg, gmat, ws2, bs_tile, *, layer, sub):
    bsz, s, d = x.shape
    tm = TOKEN_TILE
    wa_n, wb = wa.shape[1], wq.shape[1]
    const2 = lambda b, i: (0, 0)
    return pl.pallas_call(
        _hyin_kernel,
        grid=(bsz, s // tm),
        in_specs=[
            pl.BlockSpec((1, tm, d), lambda b, i: (b, i, 0)),
            _mod_spec(layer, 3 * sub + 0, d),
            _mod_spec(layer, 3 * sub + 1, d),
            pl.BlockSpec((1, 1, d), lambda b, i: (layer * 3 + sub, 0, 0)),
            _resident(wa.shape, const2),
            _resident(wq.shape, const2),
            _resident(wkt.shape, const2),
            _resident(wv.shape, const2),
            pl.BlockSpec(vg.shape, const2),
            _resident(gmat.shape, const2),
            _resident(ws2.shape, lambda b, i: (0, 0, 0)),
            pl.BlockSpec(bs_tile.shape, const2),
        ],
        out_specs=[
            pl.BlockSpec((1, tm, wa_n // 2), lambda b, i: (b, i, 0)),
            pl.BlockSpec((1, tm, wb), lambda b, i: (b, i, 0)),
            pl.BlockSpec((1, wb, tm), lambda b, i: (b, 0, i)),
            pl.BlockSpec((1, tm, wb), lambda b, i: (b, i, 0)),
        ],
        out_shape=[
            jax.ShapeDtypeStruct((bsz, s, wa_n // 2), BF16),
            jax.ShapeDtypeStruct((bsz, s, wb), BF16),
            jax.ShapeDtypeStruct((bsz, wb, s), BF16),
            jax.ShapeDtypeStruct((bsz, s, wb), BF16),
        ],
        compiler_params=_params("parallel", "parallel"),
        name="hy_in",
    )(x, mod3, mod3, ng3, wa, wq, wkt, wv, vg, gmat, ws2, bs_tile)


MASKED_LOG = -1e30
LOG2E = 1.4426950408889634
SATURATED = 105.0
SB_HEADS = 4
HEADS_PER_GROUP = LANES // HEAD_DIM


def _sb_kernel(q_ref, kt_ref, v_ref, o_ref,
               acc_ref, c_ref, cb_ref, sp_ref, lb_ref, w_ref):
    t = q_ref.shape[1]
    nh = acc_ref.shape[0]
    qi = pl.program_id(2)
    lane = lax.broadcasted_iota(jnp.int32, (t, LANES), 1)
    row = lax.broadcasted_iota(jnp.int32, (t, t), 0)
    col = lax.broadcasted_iota(jnp.int32, (t, t), 1)
    later = (row > col).astype(BF16)
    before = col < row

    def group(hh):
        g = hh // HEADS_PER_GROUP
        return slice(g * LANES, (g + 1) * LANES)

    def own_lanes(hh):
        return (lane // HEAD_DIM) == (hh % HEADS_PER_GROUP)

    q_heads = []
    for hh in range(nh):
        qg = q_ref[0, :, group(hh)]
        q_heads.append(jnp.where(own_lanes(hh), qg, jnp.zeros_like(qg)))

    def scores(k, dst, diag):
        k0 = pl.multiple_of((qi - k) * t, t)
        for hh in range(nh):
            kt = kt_ref[0, group(hh), pl.ds(k0, t)]
            z = jnp.dot(q_heads[hh], kt, preferred_element_type=F32)
            if diag:
                z = jnp.where(before, z, MASKED_LOG)
            e = jnp.exp2(jnp.abs(z) * (-LOG2E))
            sp = jnp.maximum(z, 0.0) + jnp.log(1.0 + e)
            log_b = z - sp
            sp_ref[nh * dst + hh] = sp.astype(BF16)
            lb_ref[nh * dst + hh] = log_b
            cb_ref[nh * dst + hh] = c_ref[hh]
            c_ref[hh] += jnp.sum(sp, axis=1, keepdims=True)

    def weights(src, dst):
        for hh in range(nh):
            tail = jnp.dot(sp_ref[nh * src + hh], later, preferred_element_type=F32)
            logw = (lb_ref[nh * src + hh] - tail) - cb_ref[nh * src + hh]
            w_ref[nh * dst + hh] = jnp.exp(logw).astype(BF16)

    def values(k, src, valid=None):
        k0 = pl.multiple_of((qi - k) * t, t)
        for hh in range(nh):
            vv = v_ref[0, pl.ds(k0, t), group(hh)]
            if valid is not None:
                vv = jnp.where(valid, vv, jnp.zeros_like(vv))
            acc_ref[hh] += jnp.dot(w_ref[nh * src + hh], vv, preferred_element_type=F32)

    def saturated():
        return jnp.min(c_ref[...]) >= SATURATED

    acc_ref[...] = jnp.zeros_like(acc_ref)
    c_ref[...] = jnp.zeros_like(c_ref)

    second = jnp.minimum(qi, 1)
    scores(0, 0, True)
    scores(second, 1, False)
    weights(0, 0)
    values(0, 0)
    weights(1, 1)
    values(second, 1, qi >= 1)

    rest = qi - 1

    def step(it, new):
        old = 1 - new
        scores(jnp.minimum(2 + it, qi), new, False)
        weights(old, new)
        r = it - 2
        values(jnp.clip(2 + r, 2, qi), old, jnp.logical_and(r >= 0, r < rest))

    @pl.when(jnp.logical_and(rest > 0, jnp.logical_not(saturated())))
    def _():
        scores(2, 0, False)

        def cond(state):
            p, stop = state
            return jnp.logical_and(p < (rest + 2) // 2, stop == 0)

        def body(state):
            p, _ = state
            stop = saturated().astype(jnp.int32)
            step(2 * p + 1, 1)
            step(2 * p + 2, 0)
            return p + 1, stop

        lax.while_loop(cond, body, (jnp.int32(0), jnp.int32(0)))

    for g in range(nh // HEADS_PER_GROUP):
        out = acc_ref[HEADS_PER_GROUP * g]
        for j in range(1, HEADS_PER_GROUP):
            hh = HEADS_PER_GROUP * g + j
            out = jnp.where(own_lanes(hh), acc_ref[hh], out)
        o_ref[0, :, group(HEADS_PER_GROUP * g)] = out.astype(o_ref.dtype)


def _stick_breaking(q, kt, v):
    bsz, s, wb = q.shape
    t = SB_BLOCK
    nh = SB_HEADS
    width = nh * HEAD_DIM
    return pl.pallas_call(
        _sb_kernel,
        grid=(bsz, wb // width, s // t),
        in_specs=[
            pl.BlockSpec((1, t, width), lambda b, p, i: (b, i, p)),
            pl.BlockSpec((1, width, s), lambda b, p, i: (b, p, 0)),
            pl.BlockSpec((1, s, width), lambda b, p, i: (b, 0, p)),
        ],
        out_specs=pl.BlockSpec((1, t, width), lambda b, p, i: (b, i, p)),
        out_shape=jax.ShapeDtypeStruct((bsz, s, wb), BF16),
        scratch_shapes=[
            pltpu.VMEM((nh, t, LANES), F32),
            pltpu.VMEM((nh, t, 1), F32),
            pltpu.VMEM((2 * nh, t, 1), F32),
            pltpu.VMEM((2 * nh, t, t), BF16),
            pltpu.VMEM((2 * nh, t, t), F32),
            pltpu.VMEM((2 * nh, t, t), BF16),
        ],
        compiler_params=_params("parallel", "parallel", "parallel"),
        name="stick_breaking",
    )(q, kt, v)


HALO = 8


def _sc_kernel(x_ref, sh_ref, sc_ref, g_ref, ng_ref, win_ref, cw_ref, wout_ref, o_ref, cx_ref):
    x = x_ref[0]
    tm, d = x.shape

    @pl.when(pl.program_id(1) == 0)
    def _():
        cx_ref[0:HALO, :] = jnp.zeros((HALO, d), F32)

    h = _modulated(x, ng_ref[0], sc_ref[0], sh_ref[0]).astype(BF16)
    b_gate = jnp.dot(h, win_ref[:, 0:d], preferred_element_type=F32)
    c_gate = jnp.dot(h, win_ref[:, d:2 * d], preferred_element_type=F32)
    xs = jnp.dot(h, win_ref[:, 2 * d:3 * d], preferred_element_type=F32)
    cx_ref[HALO:HALO + tm, :] = c_gate * xs
    cw = cw_ref[0]
    y = (cw[2:3] * cx_ref[HALO:HALO + tm, :]
         + cw[1:2] * cx_ref[HALO - 1:HALO - 1 + tm, :]
         + cw[0:1] * cx_ref[HALO - 2:HALO - 2 + tm, :])
    cx_ref[0:HALO, :] = cx_ref[tm:tm + HALO, :]
    m = jnp.dot((b_gate * y).astype(BF16), wout_ref[...], preferred_element_type=F32)
    o_ref[0] = x + g_ref[0] * m


def _short_conv(x, mod3, ng3, win, conv_w, wout, *, layer, sub, idx):
    bsz, s, d = x.shape
    tm = TOKEN_TILE
    return pl.pallas_call(
        _sc_kernel,
        grid=(bsz, s // tm),
        in_specs=[
            pl.BlockSpec((1, tm, d), lambda b, i: (b, i, 0)),
            _mod_spec(layer, 3 * sub + 0, d),
            _mod_spec(layer, 3 * sub + 1, d),
            _mod_spec(layer, 3 * sub + 2, d),
            pl.BlockSpec((1, 1, d), lambda b, i: (layer * 3 + sub, 0, 0)),
            _resident(win.shape, lambda b, i: (0, 0)),
            pl.BlockSpec((1,) + conv_w.shape[1:], lambda b, i: (idx, 0, 0)),
            _resident(wout.shape, lambda b, i: (0, 0)),
        ],
        out_specs=pl.BlockSpec((1, tm, d), lambda b, i: (b, i, 0)),
        out_shape=jax.ShapeDtypeStruct((bsz, s, d), F32),
        scratch_shapes=[pltpu.VMEM((tm + HALO, d), F32)],
        compiler_params=_params("parallel", "arbitrary"),
        name="short_conv",
    )(x, mod3, mod3, mod3, ng3, win, conv_w, wout)


def kernel(x, c, mod_w, mod_b, norm_g, ffn_w_gate, ffn_w_up, ffn_w_down, hy_w_in, hy_w_out,
           gm_vnorm_g, gm_w_s, gm_b_s, sc_w_in, sc_conv_w, sc_w_out, final_norm_g):
    depth = mod_w.shape[0]
    d = x.shape[-1]
    n_sub = norm_g.shape[1]

    mod = _modulation(c, mod_w, mod_b)
    mod3 = mod.reshape(depth * MOD_ROWS * N_MOD, 1, d)
    ng3 = norm_g.reshape(depth * n_sub, 1, d)

    wg, wu, wd = (w.astype(BF16) for w in (ffn_w_gate, ffn_w_up, ffn_w_down))

    for layer in range(depth):
        i = layer // 2
        x = _ffn(x, mod3, ng3, wg, wu, wd, final_norm_g,
                 layer=layer, sub=0, which=0, final=False)
        if layer % 2 == 0:
            n_ha = gm_w_s.shape[1]
            wa_n = 2 * n_ha * HEAD_DIM
            wb = (hy_w_in.shape[2] - wa_n) // 3
            w_in = hy_w_in[i].astype(BF16)
            wa = w_in[:, :wa_n]
            wq = w_in[:, wa_n:wa_n + wb]
            wkt = w_in[:, wa_n + wb:wa_n + 2 * wb].T
            wv = w_in[:, wa_n + 2 * wb:]
            head_of = jnp.arange(wa_n // 2) // HEAD_DIM
            gmat = (head_of[:, None] == head_of[None, :]).astype(BF16)
            ws2 = gm_w_s[i].astype(BF16).reshape(n_ha // 2, 2 * CHUNK, CHUNK)
            bs_tile = jnp.repeat(gm_b_s[i].T, HEAD_DIM, axis=1)
            ya, q, kt, v = _hy_in(x, mod3, ng3, wa, wq, wkt, wv, gm_vnorm_g[i][None, :],
                                  gmat, ws2, bs_tile, layer=layer, sub=1)
            yb = _stick_breaking(q, kt, v)
            mix = (ya, yb, hy_w_out[i].astype(BF16))
        else:
            x = _short_conv(x, mod3, ng3, sc_w_in[i].astype(BF16), sc_conv_w,
                            sc_w_out[i].astype(BF16), layer=layer, sub=1, idx=i)
            mix = None
        x = _ffn(x, mod3, ng3, wg, wu, wd, final_norm_g,
                 layer=layer, sub=2, which=1, final=(layer == depth - 1), mix=mix)
    return x
```

```python
import functools

import jax
import jax.numpy as jnp
from jax import lax
from jax.experimental import pallas as pl
from jax.experimental.pallas import tpu as pltpu

F32 = jnp.float32
BF16 = jnp.bfloat16

HEAD_DIM = 64
CHUNK = 128
EPS = 1e-6
MACARON_WEIGHT = 0.5
N_MOD = 9
MOD_ROWS = 8

LANES = 128
BF16_ROWS = 16
SB_BLOCK = 256
TOKEN_TILE = 1024
VMEM_LIMIT = 56 * 1024 * 1024


def _resident(block_shape, index_map):
    return pl.BlockSpec(block_shape, index_map, pipeline_mode=pl.Buffered(1))


def _params(*semantics):
    return pltpu.CompilerParams(dimension_semantics=semantics,
                                vmem_limit_bytes=VMEM_LIMIT)


def _rms(x):
    return lax.rsqrt(jnp.mean(x * x, axis=-1, keepdims=True) + EPS)


def _modulated(x, ng, sc, sh):
    return (x * _rms(x)) * (ng * (1.0 + sc)) + sh


def _silu(x):
    return x * jax.nn.sigmoid(x)


def _mod_kernel(c_ref, w_ref, b_ref, o_ref):
    cond = _silu(c_ref[...]).astype(BF16)
    o_ref[0] = jnp.dot(cond, w_ref[0].astype(BF16),
                       preferred_element_type=F32) + b_ref[0]


def _modulation(c, mod_w, mod_b):
    depth, d, n = mod_w.shape
    bsz = c.shape[0]
    tn = 1024
    c8 = jnp.zeros((MOD_ROWS, d), F32).at[:bsz].set(c)
    return pl.pallas_call(
        _mod_kernel,
        grid=(depth, n // tn),
        in_specs=[
            pl.BlockSpec((MOD_ROWS, d), lambda l, j: (0, 0)),
            pl.BlockSpec((1, d, tn), lambda l, j: (l, 0, j)),
            pl.BlockSpec((1, 1, tn), lambda l, j: (l, 0, j)),
        ],
        out_specs=pl.BlockSpec((1, MOD_ROWS, tn), lambda l, j: (l, 0, j)),
        out_shape=jax.ShapeDtypeStruct((depth, MOD_ROWS, n), F32),
        compiler_params=_params("parallel", "parallel"),
        name="adaln_mod",
    )(c8, mod_w, mod_b.reshape(depth, 1, n))


def _mod_spec(layer, k, d):
    base = layer * MOD_ROWS * N_MOD + k
    return pl.BlockSpec((1, 1, d), lambda b, i: (base + b * N_MOD, 0, 0))


FF_CHUNK = 256


def _ffn_kernel(*refs, final, mixed):
    if mixed:
        x_ref, ya_ref, yb_ref, wo_ref, gm_ref = refs[:5]
        wa = ya_ref.shape[2]
        m = (jnp.dot(ya_ref[0], wo_ref[:wa, :], preferred_element_type=F32)
             + jnp.dot(yb_ref[0], wo_ref[wa:, :], preferred_element_type=F32))
        x = x_ref[0] + gm_ref[0] * m
        refs = refs[5:]
    else:
        x = refs[0][0]
        refs = refs[1:]
    sh_ref, sc_ref, g_ref, ng_ref, wg_ref, wu_ref, wd_ref, fg_ref, o_ref, a_ref = refs
    h = _modulated(x, ng_ref[0], sc_ref[0], sh_ref[0]).astype(BF16)
    d_ff = a_ref.shape[1]
    for c0 in range(0, d_ff, FF_CHUNK):
        sl = slice(c0, c0 + FF_CHUNK)
        g = jnp.dot(h, wg_ref[:, sl], preferred_element_type=F32)
        u = jnp.dot(h, wu_ref[:, sl], preferred_element_type=F32)
        a_ref[:, sl] = (_silu(g) * u).astype(BF16)
    y = jnp.dot(a_ref[...], wd_ref[...], preferred_element_type=F32)
    out = x + (MACARON_WEIGHT * g_ref[0]) * y
    if final:
        out = (out * _rms(out)) * fg_ref[...]
    o_ref[0] = out


def _ffn(x, mod3, ng3, weights, final_g, *, layer, sub, final, mix=None):
    bsz, s, d = x.shape
    wg, wu, wd, f = weights
    d_ff = wg.shape[-1]
    tm = TOKEN_TILE
    tile = lambda b, i: (b, i, 0)
    mix_specs, mix_args = [], []
    if mix is not None:
        ya, yb, wo = mix
        mix_specs = [
            pl.BlockSpec((1, tm, ya.shape[2]), tile),
            pl.BlockSpec((1, tm, yb.shape[2]), tile),
            _resident(wo.shape, lambda b, i: (0, 0)),
            _mod_spec(layer, 3 * (sub - 1) + 2, d),
        ]
        mix_args = [ya, yb, wo, mod3]
    return pl.pallas_call(
        functools.partial(_ffn_kernel, final=final, mixed=mix is not None),
        grid=(bsz, s // tm),
        in_specs=[
            pl.BlockSpec((1, tm, d), tile),
            *mix_specs,
            _mod_spec(layer, 3 * sub + 0, d),
            _mod_spec(layer, 3 * sub + 1, d),
            _mod_spec(layer, 3 * sub + 2, d),
            pl.BlockSpec((1, 1, d), lambda b, i: (layer * 3 + sub, 0, 0)),
            _resident((d, d_ff), lambda b, i: (f, 0)),
            _resident((d, d_ff), lambda b, i: (f, 0)),
            _resident((d_ff, d), lambda b, i: (f, 0)),
            pl.BlockSpec((1, d), lambda b, i: (0, 0)),
        ],
        out_specs=pl.BlockSpec((1, tm, d), tile),
        out_shape=jax.ShapeDtypeStruct((bsz, s, d), F32),
        scratch_shapes=[pltpu.VMEM((tm, d_ff), BF16)],
        compiler_params=_params("parallel", "parallel"),
        name=f"ffn_l{layer}_s{sub}",
    )(x, *mix_args, mod3, mod3, mod3, ng3, wg, wu, wd, final_g.reshape(1, d))


def _hyin_kernel(*refs, n_cast):
    (x_ref, sh_ref, sc_ref, ng_ref, wa_ref, wq_ref, wkt_ref, wv_ref, vg_ref,
     gmat_ref, ws_ref, bs_ref) = refs[:12]
    cast_in = refs[12:12 + n_cast]
    ya_ref, q_ref, kt_ref, v_ref = refs[12 + n_cast:16 + n_cast]
    cast_out = refs[16 + n_cast:]
    for src, dst in zip(cast_in, cast_out):
        dst[...] = src[...].astype(BF16)

    x = x_ref[0]
    tm = x.shape[0]
    h = _modulated(x, ng_ref[0], sc_ref[0], sh_ref[0]).astype(BF16)

    uv = jnp.dot(h, wa_ref[...], preferred_element_type=F32)
    q = jnp.dot(h, wq_ref[...], preferred_element_type=F32)
    q_ref[0] = (q * (HEAD_DIM ** -0.5)).astype(BF16)
    uv = 0.5 * uv * (1.0 + lax.erf(uv * (2.0 ** -0.5)))
    wa = uv.shape[1] // 2
    u, v = uv[:, :wa], uv[:, wa:]
    ssq = jnp.dot((v * v).astype(BF16), gmat_ref[...], preferred_element_type=F32)
    v_ref[0] = jnp.dot(h, wv_ref[...], preferred_element_type=F32).astype(BF16)
    vn = (v * lax.rsqrt(ssq * (1.0 / HEAD_DIM) + EPS) * vg_ref[...]).astype(BF16)

    n_pairs = wa // LANES
    r2 = lax.broadcasted_iota(jnp.int32, (2 * CHUNK, CHUNK), 0)
    c2 = lax.broadcasted_iota(jnp.int32, (2 * CHUNK, CHUNK), 1)
    causal = (r2 & (CHUNK - 1)) >= c2
    first_head = lax.broadcasted_iota(jnp.int32, (CHUNK, LANES), 1) < HEAD_DIM
    w_pairs = [jnp.where(causal, ws_ref[p], jnp.zeros_like(ws_ref[p])) for p in range(n_pairs)]
    for ci in range(0, tm // CHUNK, 2):
        rows = [slice((ci + j) * CHUNK, (ci + j + 1) * CHUNK) for j in range(2)]
        parts = [[], []]
        for p in range(n_pairs):
            lanes = slice(p * LANES, (p + 1) * LANES)
            vp = jnp.concatenate([vn[rows[0], lanes], vn[rows[1], lanes]], axis=1)
            r = jnp.dot(w_pairs[p], vp, preferred_element_type=F32)
            for j in range(2):
                rj = r[:, j * LANES:(j + 1) * LANES]
                parts[j].append(jnp.where(first_head, rj[:CHUNK], rj[CHUNK:]))
        for j in range(2):
            sv = jnp.concatenate(parts[j], axis=1) + bs_ref[...]
            ya_ref[0, rows[j], :] = (u[rows[j]] * sv).astype(BF16)
    kt_ref[0] = lax.dot_general(wkt_ref[...], h, (((1,), (1,)), ((), ())),
                                preferred_element_type=F32).astype(BF16)


def _hy_in(x, mod3, ng3, wa, wq, wkt, wv, vg, gmat, ws2, bs_tile, cast, *, layer, sub):
    bsz, s, d = x.shape
    tm = TOKEN_TILE
    n_i = s // tm
    wa_n, wb = wa.shape[1], wq.shape[1]
    const2 = lambda b, i: (0, 0)

    def cast_specs(w, start, count):
        rb = count // (bsz * n_i)
        assert rb * bsz * n_i == count and rb % BF16_ROWS == 0 and start % rb == 0
        cols = w.shape[1]
        return (pl.BlockSpec((rb, cols), lambda b, i: (start // rb + b * n_i + i, 0)),
                pl.BlockSpec((rb, cols), lambda b, i: (b * n_i + i, 0)),
                jax.ShapeDtypeStruct((count, cols), BF16))

    casts = [cast_specs(*c) for c in cast]
    return pl.pallas_call(
        functools.partial(_hyin_kernel, n_cast=len(casts)),
        grid=(bsz, n_i),
        in_specs=[
            pl.BlockSpec((1, tm, d), lambda b, i: (b, i, 0)),
            _mod_spec(layer, 3 * sub + 0, d),
            _mod_spec(layer, 3 * sub + 1, d),
            pl.BlockSpec((1, 1, d), lambda b, i: (layer * 3 + sub, 0, 0)),
            _resident(wa.shape, const2),
            _resident(wq.shape, const2),
            _resident(wkt.shape, const2),
            _resident(wv.shape, const2),
            pl.BlockSpec(vg.shape, const2),
            _resident(gmat.shape, const2),
            _resident(ws2.shape, lambda b, i: (0, 0, 0)),
            pl.BlockSpec(bs_tile.shape, const2),
            *[c[0] for c in casts],
        ],
        out_specs=[
            pl.BlockSpec((1, tm, wa_n // 2), lambda b, i: (b, i, 0)),
            pl.BlockSpec((1, tm, wb), lambda b, i: (b, i, 0)),
            pl.BlockSpec((1, wb, tm), lambda b, i: (b, 0, i)),
            pl.BlockSpec((1, tm, wb), lambda b, i: (b, i, 0)),
            *[c[1] for c in casts],
        ],
        out_shape=[
            jax.ShapeDtypeStruct((bsz, s, wa_n // 2), BF16),
            jax.ShapeDtypeStruct((bsz, s, wb), BF16),
            jax.ShapeDtypeStruct((bsz, wb, s), BF16),
            jax.ShapeDtypeStruct((bsz, s, wb), BF16),
            *[c[2] for c in casts],
        ],
        compiler_params=_params("parallel", "parallel"),
        name="hy_in",
    )(x, mod3, mod3, ng3, wa, wq, wkt, wv, vg, gmat, ws2, bs_tile, *[c[0] for c in cast])


MASKED_LOG = -1e30
LOG2E = 1.4426950408889634
SATURATED = 105.0
SB_HEADS = 8
HEADS_PER_GROUP = LANES // HEAD_DIM


def _sb_kernel(q_ref, kt_ref, v_ref, o_ref,
               acc_ref, c_ref, cb_ref, sp_ref, lb_ref, w_ref):
    t = q_ref.shape[1]
    nh = acc_ref.shape[0]
    qi = pl.program_id(2)
    lane = lax.broadcasted_iota(jnp.int32, (t, LANES), 1)
    row = lax.broadcasted_iota(jnp.int32, (t, t), 0)
    col = lax.broadcasted_iota(jnp.int32, (t, t), 1)
    later = (row > col).astype(BF16)
    before = col < row

    def group(hh):
        g = hh // HEADS_PER_GROUP
        return slice(g * LANES, (g + 1) * LANES)

    def own_lanes(hh):
        return (lane // HEAD_DIM) == (hh % HEADS_PER_GROUP)

    q_heads = []
    for hh in range(nh):
        qg = q_ref[0, :, group(hh)]
        q_heads.append(jnp.where(own_lanes(hh), qg, jnp.zeros_like(qg)))

    def scores(k, dst, diag):
        k0 = pl.multiple_of((qi - k) * t, t)
        for hh in range(nh):
            kt = kt_ref[0, group(hh), pl.ds(k0, t)]
            z = jnp.dot(q_heads[hh], kt, preferred_element_type=F32)
            if diag:
                z = jnp.where(before, z, MASKED_LOG)
            e = jnp.exp2(jnp.abs(z) * (-LOG2E))
            sp = jnp.maximum(z, 0.0) + jnp.log(1.0 + e)
            log_b = z - sp
            sp_ref[nh * dst + hh] = sp.astype(BF16)
            lb_ref[nh * dst + hh] = log_b
            cb_ref[nh * dst + hh] = c_ref[hh]
            c_ref[hh] += jnp.sum(sp, axis=1, keepdims=True)

    def weights(src, dst):
        for hh in range(nh):
            tail = jnp.dot(sp_ref[nh * src + hh], later, preferred_element_type=F32)
            logw = (lb_ref[nh * src + hh] - tail) - cb_ref[nh * src + hh]
            w_ref[nh * dst + hh] = jnp.exp(logw).astype(BF16)

    def values(k, src, valid=None):
        k0 = pl.multiple_of((qi - k) * t, t)
        for hh in range(nh):
            vv = v_ref[0, pl.ds(k0, t), group(hh)]
            if valid is not None:
                vv = jnp.where(valid, vv, jnp.zeros_like(vv))
            acc_ref[hh] += jnp.dot(w_ref[nh * src + hh], vv, preferred_element_type=F32)

    def saturated():
        return jnp.min(c_ref[...]) >= SATURATED

    acc_ref[...] = jnp.zeros_like(acc_ref)
    c_ref[...] = jnp.zeros_like(c_ref)

    second = jnp.minimum(qi, 1)
    scores(0, 0, True)
    scores(second, 1, False)
    weights(0, 0)
    values(0, 0)
    weights(1, 1)
    values(second, 1, qi >= 1)

    rest = qi - 1

    def step(it, new):
        old = 1 - new
        scores(jnp.minimum(2 + it, qi), new, False)
        weights(old, new)
        r = it - 2
        values(jnp.clip(2 + r, 2, qi), old, jnp.logical_and(r >= 0, r < rest))

    @pl.when(jnp.logical_and(rest > 0, jnp.logical_not(saturated())))
    def _():
        scores(2, 0, False)

        def cond(state):
            p, stop = state
            return jnp.logical_and(p < (rest + 2) // 2, stop == 0)

        def body(state):
            p, _ = state
            stop = saturated().astype(jnp.int32)
            step(2 * p + 1, 1)
            step(2 * p + 2, 0)
            return p + 1, stop

        lax.while_loop(cond, body, (jnp.int32(0), jnp.int32(0)))

    for g in range(nh // HEADS_PER_GROUP):
        out = acc_ref[HEADS_PER_GROUP * g]
        for j in range(1, HEADS_PER_GROUP):
            hh = HEADS_PER_GROUP * g + j
            out = jnp.where(own_lanes(hh), acc_ref[hh], out)
        o_ref[0, :, group(HEADS_PER_GROUP * g)] = out.astype(o_ref.dtype)


def _stick_breaking(q, kt, v):
    bsz, s, wb = q.shape
    t = SB_BLOCK
    nh = SB_HEADS
    width = nh * HEAD_DIM
    return pl.pallas_call(
        _sb_kernel,
        grid=(bsz, wb // width, s // t),
        in_specs=[
            pl.BlockSpec((1, t, width), lambda b, p, i: (b, i, p)),
            pl.BlockSpec((1, width, s), lambda b, p, i: (b, p, 0)),
            pl.BlockSpec((1, s, width), lambda b, p, i: (b, 0, p)),
        ],
        out_specs=pl.BlockSpec((1, t, width), lambda b, p, i: (b, i, p)),
        out_shape=jax.ShapeDtypeStruct((bsz, s, wb), BF16),
        scratch_shapes=[
            pltpu.VMEM((nh, t, LANES), F32),
            pltpu.VMEM((nh, t, 1), F32),
            pltpu.VMEM((2 * nh, t, 1), F32),
            pltpu.VMEM((2 * nh, t, t), BF16),
            pltpu.VMEM((2 * nh, t, t), F32),
            pltpu.VMEM((2 * nh, t, t), BF16),
        ],
        compiler_params=_params("parallel", "parallel", "parallel"),
        name="stick_breaking",
    )(q, kt, v)


HALO = 8


def _sc_kernel(x_ref, sh_ref, sc_ref, g_ref, ng_ref, win_ref, cw_ref, wout_ref, o_ref,
               cx_ref, z_ref):
    x = x_ref[0]
    tm, d = x.shape

    @pl.when(pl.program_id(1) == 0)
    def _():
        cx_ref[0:HALO, :] = jnp.zeros((HALO, d), F32)

    h = _modulated(x, ng_ref[0], sc_ref[0], sh_ref[0]).astype(BF16)
    cw = cw_ref[0]
    for c0 in range(0, d, FF_CHUNK):
        sl = slice(c0, c0 + FF_CHUNK)
        b_gate = jnp.dot(h, win_ref[:, sl], preferred_element_type=F32)
        c_gate = jnp.dot(h, win_ref[:, d + c0:d + c0 + FF_CHUNK], preferred_element_type=F32)
        xs = jnp.dot(h, win_ref[:, 2 * d + c0:2 * d + c0 + FF_CHUNK],
                     preferred_element_type=F32)
        cx_ref[HALO:HALO + tm, sl] = c_gate * xs
        y = (cw[2:3, sl] * cx_ref[HALO:HALO + tm, sl]
             + cw[1:2, sl] * cx_ref[HALO - 1:HALO - 1 + tm, sl]
             + cw[0:1, sl] * cx_ref[HALO - 2:HALO - 2 + tm, sl])
        z_ref[:, sl] = (b_gate * y).astype(BF16)
    cx_ref[0:HALO, :] = cx_ref[tm:tm + HALO, :]
    m = jnp.dot(z_ref[...], wout_ref[...], preferred_element_type=F32)
    o_ref[0] = x + g_ref[0] * m


def _short_conv(x, mod3, ng3, win, conv_w, wout, *, layer, sub, idx):
    bsz, s, d = x.shape
    tm = TOKEN_TILE
    return pl.pallas_call(
        _sc_kernel,
        grid=(bsz, s // tm),
        in_specs=[
            pl.BlockSpec((1, tm, d), lambda b, i: (b, i, 0)),
            _mod_spec(layer, 3 * sub + 0, d),
            _mod_spec(layer, 3 * sub + 1, d),
            _mod_spec(layer, 3 * sub + 2, d),
            pl.BlockSpec((1, 1, d), lambda b, i: (layer * 3 + sub, 0, 0)),
            _resident(win.shape, lambda b, i: (0, 0)),
            pl.BlockSpec((1,) + conv_w.shape[1:], lambda b, i: (idx, 0, 0)),
            _resident(wout.shape, lambda b, i: (0, 0)),
        ],
        out_specs=pl.BlockSpec((1, tm, d), lambda b, i: (b, i, 0)),
        out_shape=jax.ShapeDtypeStruct((bsz, s, d), F32),
        scratch_shapes=[pltpu.VMEM((tm + HALO, d), F32),
                        pltpu.VMEM((tm, d), BF16)],
        compiler_params=_params("parallel", "arbitrary"),
        name="short_conv",
    )(x, mod3, mod3, mod3, ng3, win, conv_w, wout)


def kernel(x, c, mod_w, mod_b, norm_g, ffn_w_gate, ffn_w_up, ffn_w_down, hy_w_in, hy_w_out,
           gm_vnorm_g, gm_w_s, gm_b_s, sc_w_in, sc_conv_w, sc_w_out, final_norm_g):
    depth = mod_w.shape[0]
    assert depth == 2, "layer 0's mixer hosts the weight casts of every later FFN"
    d = x.shape[-1]
    d_ff = ffn_w_gate.shape[-1]
    n_sub = norm_g.shape[1]

    mod = _modulation(c, mod_w, mod_b)
    mod3 = mod.reshape(depth * MOD_ROWS * N_MOD, 1, d)
    ng3 = norm_g.reshape(depth * n_sub, 1, d)

    ffn_f32 = [w.reshape(-1, w.shape[-1]) for w in (ffn_w_gate, ffn_w_up, ffn_w_down)]
    ffn_rows = (d, d, d_ff)
    ffn_w = [(*[w[0, 0].astype(BF16) for w in (ffn_w_gate, ffn_w_up, ffn_w_down)], 0)]
    later_f32 = ([(w, r, r) for w, r in zip(ffn_f32, ffn_rows)]
                 + [(w, 2 * r, 2 * r) for w, r in zip(ffn_f32, ffn_rows)]
                 + [(w[0], 0, w.shape[1]) for w in (hy_w_out, sc_w_in, sc_w_out)])

    for layer in range(depth):
        i = layer // 2
        x = _ffn(x, mod3, ng3, ffn_w[2 * layer], final_norm_g, layer=layer, sub=0, final=False)
        if layer % 2 == 0:
            n_ha = gm_w_s.shape[1]
            wa_n = 2 * n_ha * HEAD_DIM
            wb = (hy_w_in.shape[2] - wa_n) // 3
            w_in = hy_w_in[i].astype(BF16)
            wa = w_in[:, :wa_n]
            wq = w_in[:, wa_n:wa_n + wb]
            wkt = w_in[:, wa_n + wb:wa_n + 2 * wb].T
            wv = w_in[:, wa_n + 2 * wb:]
            head_of = jnp.arange(wa_n // 2) // HEAD_DIM
            gmat = (head_of[:, None] == head_of[None, :]).astype(BF16)
            ws2 = gm_w_s[i].astype(BF16).reshape(n_ha // 2, 2 * CHUNK, CHUNK)
            bs_tile = jnp.repeat(gm_b_s[i].T, HEAD_DIM, axis=1)
            ya, q, kt, v, *cast = _hy_in(x, mod3, ng3, wa, wq, wkt, wv, gm_vnorm_g[i][None, :],
                                         gmat, ws2, bs_tile, later_f32, layer=layer, sub=1)
            ffn_w += [(*cast[0:3], 0), (*cast[3:6], 0), (*cast[3:6], 1)]
            hy_wo, sc_wi, sc_wo = cast[6:9]
            yb = _stick_breaking(q, kt, v)
            mix = (ya, yb, hy_wo)
        else:
            x = _short_conv(x, mod3, ng3, sc_wi, sc_conv_w, sc_wo, layer=layer, sub=1, idx=i)
            mix = None
        x = _ffn(x, mod3, ng3, ffn_w[2 * layer + 1], final_norm_g,
                 layer=layer, sub=2, final=(layer == depth - 1), mix=mix)
    return x
```

```python
import functools

import jax
import jax.numpy as jnp
from jax import lax
from jax.experimental import pallas as pl
from jax.experimental.pallas import tpu as pltpu

F32 = jnp.float32
BF16 = jnp.bfloat16

HEAD_DIM = 64
CHUNK = 128
EPS = 1e-6
MACARON_WEIGHT = 0.5
N_MOD = 9
MOD_ROWS = 8

LANES = 128
BF16_ROWS = 16
SB_BLOCK = 256
TOKEN_TILE = 1024
VMEM_LIMIT = 56 * 1024 * 1024


def _resident(block_shape, index_map):
    return pl.BlockSpec(block_shape, index_map, pipeline_mode=pl.Buffered(1))


def _params(*semantics):
    return pltpu.CompilerParams(dimension_semantics=semantics,
                                vmem_limit_bytes=VMEM_LIMIT)


def _rms(x):
    return lax.rsqrt(jnp.mean(x * x, axis=-1, keepdims=True) + EPS)


def _modulated(x, ng, sc, sh):
    return (x * _rms(x)) * (ng * (1.0 + sc)) + sh


def _silu(x):
    return x * jax.nn.sigmoid(x)


def _mod_kernel(c_ref, w_ref, b_ref, o_ref):
    cond = _silu(c_ref[...]).astype(BF16)
    res = jnp.dot(cond, w_ref[0].astype(BF16), preferred_element_type=F32) + b_ref[0]
    for b in range(MOD_ROWS):
        o_ref[b] = res[b:b + 1, :]


def _modulation(c, mod_w, mod_b):
    depth, d, n = mod_w.shape
    bsz = c.shape[0]
    assert n == N_MOD * d
    c8 = jnp.zeros((MOD_ROWS, d), F32).at[:bsz].set(c)
    return pl.pallas_call(
        _mod_kernel,
        grid=(depth, N_MOD),
        in_specs=[
            pl.BlockSpec((MOD_ROWS, d), lambda l, j: (0, 0)),
            pl.BlockSpec((1, d, d), lambda l, j: (l, 0, j)),
            pl.BlockSpec((1, 1, d), lambda l, j: (l, 0, j)),
        ],
        out_specs=pl.BlockSpec((MOD_ROWS, 1, d), lambda l, j: (l * N_MOD + j, 0, 0)),
        out_shape=jax.ShapeDtypeStruct((depth * N_MOD * MOD_ROWS, 1, d), F32),
        compiler_params=_params("parallel", "parallel"),
        name="adaln_mod",
    )(c8, mod_w, mod_b.reshape(depth, 1, n))


def _mod_spec(layer, k, d):
    base = (layer * N_MOD + k) * MOD_ROWS
    return pl.BlockSpec((1, 1, d), lambda b, i: (base + b, 0, 0))


FF_CHUNK = 256


def _ffn_kernel(*refs, final, mixed):
    if mixed:
        x_ref, ya_ref, yb_ref, wo_ref, gm_ref = refs[:5]
        wa = ya_ref.shape[2]
        m = (jnp.dot(ya_ref[0], wo_ref[:wa, :], preferred_element_type=F32)
             + jnp.dot(yb_ref[0], wo_ref[wa:, :], preferred_element_type=F32))
        x = x_ref[0] + gm_ref[0] * m
        refs = refs[5:]
    else:
        x = refs[0][0]
        refs = refs[1:]
    sh_ref, sc_ref, g_ref, ng_ref, wg_ref, wu_ref, wd_ref, fg_ref, o_ref, a_ref = refs
    h = _modulated(x, ng_ref[0], sc_ref[0], sh_ref[0]).astype(BF16)
    d_ff = a_ref.shape[1]
    for c0 in range(0, d_ff, FF_CHUNK):
        sl = slice(c0, c0 + FF_CHUNK)
        g = jnp.dot(h, wg_ref[:, sl], preferred_element_type=F32)
        u = jnp.dot(h, wu_ref[:, sl], preferred_element_type=F32)
        a_ref[:, sl] = (_silu(g) * u).astype(BF16)
    y = jnp.dot(a_ref[...], wd_ref[...], preferred_element_type=F32)
    out = x + (MACARON_WEIGHT * g_ref[0]) * y
    if final:
        out = (out * _rms(out)) * fg_ref[...]
    o_ref[0] = out


def _ffn(x, mod3, ng3, weights, final_g, *, layer, sub, final, mix=None):
    bsz, s, d = x.shape
    wg, wu, wd, f = weights
    d_ff = wg.shape[-1]
    tm = TOKEN_TILE
    tile = lambda b, i: (b, i, 0)
    mix_specs, mix_args = [], []
    if mix is not None:
        ya, yb, wo = mix
        mix_specs = [
            pl.BlockSpec((1, tm, ya.shape[2]), tile),
            pl.BlockSpec((1, tm, yb.shape[2]), tile),
            _resident(wo.shape, lambda b, i: (0, 0)),
            _mod_spec(layer, 3 * (sub - 1) + 2, d),
        ]
        mix_args = [ya, yb, wo, mod3]
    return pl.pallas_call(
        functools.partial(_ffn_kernel, final=final, mixed=mix is not None),
        grid=(bsz, s // tm),
        in_specs=[
            pl.BlockSpec((1, tm, d), tile),
            *mix_specs,
            _mod_spec(layer, 3 * sub + 0, d),
            _mod_spec(layer, 3 * sub + 1, d),
            _mod_spec(layer, 3 * sub + 2, d),
            pl.BlockSpec((1, 1, d), lambda b, i: (layer * 3 + sub, 0, 0)),
            _resident((d, d_ff), lambda b, i: (f, 0)),
            _resident((d, d_ff), lambda b, i: (f, 0)),
            _resident((d_ff, d), lambda b, i: (f, 0)),
            pl.BlockSpec((1, d), lambda b, i: (0, 0)),
        ],
        out_specs=pl.BlockSpec((1, tm, d), tile),
        out_shape=jax.ShapeDtypeStruct((bsz, s, d), F32),
        scratch_shapes=[pltpu.VMEM((tm, d_ff), BF16)],
        compiler_params=_params("parallel", "parallel"),
        name=f"ffn_l{layer}_s{sub}",
    )(x, *mix_args, mod3, mod3, mod3, ng3, wg, wu, wd, final_g.reshape(1, d))


def _hyin_kernel(*refs, n_cast):
    (x_ref, sh_ref, sc_ref, ng_ref, wa_ref, wq_ref, wkt_ref, wv_ref, vg_ref,
     gmat_ref, ws_ref, bs_ref) = refs[:12]
    cast_in = refs[12:12 + n_cast]
    ya_ref, q_ref, kt_ref, v_ref = refs[12 + n_cast:16 + n_cast]
    cast_out = refs[16 + n_cast:]
    for src, dst in zip(cast_in, cast_out):
        dst[...] = src[...].astype(BF16)

    x = x_ref[0]
    tm = x.shape[0]
    h = _modulated(x, ng_ref[0], sc_ref[0], sh_ref[0]).astype(BF16)

    uv = jnp.dot(h, wa_ref[...], preferred_element_type=F32)
    q = jnp.dot(h, wq_ref[...], preferred_element_type=F32)
    q_ref[0] = (q * (HEAD_DIM ** -0.5)).astype(BF16)
    uv = 0.5 * uv * (1.0 + lax.erf(uv * (2.0 ** -0.5)))
    wa = uv.shape[1] // 2
    u, v = uv[:, :wa], uv[:, wa:]
    ssq = jnp.dot((v * v).astype(BF16), gmat_ref[...], preferred_element_type=F32)
    v_ref[0] = jnp.dot(h, wv_ref[...], preferred_element_type=F32).astype(BF16)
    vn = (v * lax.rsqrt(ssq * (1.0 / HEAD_DIM) + EPS) * vg_ref[...]).astype(BF16)

    n_pairs = wa // LANES
    r2 = lax.broadcasted_iota(jnp.int32, (2 * CHUNK, CHUNK), 0)
    c2 = lax.broadcasted_iota(jnp.int32, (2 * CHUNK, CHUNK), 1)
    causal = (r2 & (CHUNK - 1)) >= c2
    first_head = lax.broadcasted_iota(jnp.int32, (CHUNK, LANES), 1) < HEAD_DIM
    w_pairs = [jnp.where(causal, ws_ref[p], jnp.zeros_like(ws_ref[p])) for p in range(n_pairs)]
    for ci in range(0, tm // CHUNK, 2):
        rows = [slice((ci + j) * CHUNK, (ci + j + 1) * CHUNK) for j in range(2)]
        parts = [[], []]
        for p in range(n_pairs):
            lanes = slice(p * LANES, (p + 1) * LANES)
            vp = jnp.concatenate([vn[rows[0], lanes], vn[rows[1], lanes]], axis=1)
            r = jnp.dot(w_pairs[p], vp, preferred_element_type=F32)
            for j in range(2):
                rj = r[:, j * LANES:(j + 1) * LANES]
                parts[j].append(jnp.where(first_head, rj[:CHUNK], rj[CHUNK:]))
        for j in range(2):
            sv = jnp.concatenate(parts[j], axis=1) + bs_ref[...]
            ya_ref[0, rows[j], :] = (u[rows[j]] * sv).astype(BF16)
    kt_ref[0] = lax.dot_general(wkt_ref[...], h, (((1,), (1,)), ((), ())),
                                preferred_element_type=F32).astype(BF16)


def _hy_in(x, mod3, ng3, w_in, wkt, vg, gmat, ws2, bs_tile, cast, *, layer, sub):
    bsz, s, d = x.shape
    tm = TOKEN_TILE
    n_i = s // tm
    wb = wkt.shape[0]
    wa_n = w_in.shape[1] - 3 * wb
    assert wa_n % wb == 0
    const2 = lambda b, i: (0, 0)

    def cast_specs(w, start, count):
        rb = count // (bsz * n_i)
        assert rb * bsz * n_i == count and rb % BF16_ROWS == 0 and start % rb == 0
        cols = w.shape[1]
        return (pl.BlockSpec((rb, cols), lambda b, i: (start // rb + b * n_i + i, 0)),
                pl.BlockSpec((rb, cols), lambda b, i: (b * n_i + i, 0)),
                jax.ShapeDtypeStruct((count, cols), BF16))

    casts = [cast_specs(*c) for c in cast]
    return pl.pallas_call(
        functools.partial(_hyin_kernel, n_cast=len(casts)),
        grid=(bsz, n_i),
        in_specs=[
            pl.BlockSpec((1, tm, d), lambda b, i: (b, i, 0)),
            _mod_spec(layer, 3 * sub + 0, d),
            _mod_spec(layer, 3 * sub + 1, d),
            pl.BlockSpec((1, 1, d), lambda b, i: (layer * 3 + sub, 0, 0)),
            _resident((d, wa_n), const2),
            _resident((d, wb), lambda b, i: (0, wa_n // wb)),
            _resident(wkt.shape, const2),
            _resident((d, wb), lambda b, i: (0, wa_n // wb + 2)),
            pl.BlockSpec(vg.shape, const2),
            _resident(gmat.shape, const2),
            _resident(ws2.shape, lambda b, i: (0, 0, 0)),
            pl.BlockSpec(bs_tile.shape, const2),
            *[c[0] for c in casts],
        ],
        out_specs=[
            pl.BlockSpec((1, tm, wa_n // 2), lambda b, i: (b, i, 0)),
            pl.BlockSpec((1, tm, wb), lambda b, i: (b, i, 0)),
            pl.BlockSpec((1, wb, tm), lambda b, i: (b, 0, i)),
            pl.BlockSpec((1, tm, wb), lambda b, i: (b, i, 0)),
            *[c[1] for c in casts],
        ],
        out_shape=[
            jax.ShapeDtypeStruct((bsz, s, wa_n // 2), BF16),
            jax.ShapeDtypeStruct((bsz, s, wb), BF16),
            jax.ShapeDtypeStruct((bsz, wb, s), BF16),
            jax.ShapeDtypeStruct((bsz, s, wb), BF16),
            *[c[2] for c in casts],
        ],
        compiler_params=_params("parallel", "parallel"),
        name="hy_in",
    )(x, mod3, mod3, ng3, w_in, w_in, wkt, w_in, vg, gmat, ws2, bs_tile, *[c[0] for c in cast])


MASKED_LOG = -1e30
LOG2E = 1.4426950408889634
SATURATED = 105.0
SB_HEADS = 4
HEADS_PER_GROUP = LANES // HEAD_DIM


def _sb_kernel(q_ref, kt_ref, v_ref, o_ref,
               acc_ref, c_ref, cb_ref, sp_ref, lb_ref, w_ref):
    t = q_ref.shape[1]
    nh = acc_ref.shape[0]
    qi = pl.program_id(2)
    lane = lax.broadcasted_iota(jnp.int32, (t, LANES), 1)
    row = lax.broadcasted_iota(jnp.int32, (t, t), 0)
    col = lax.broadcasted_iota(jnp.int32, (t, t), 1)
    later = (row > col).astype(BF16)
    before = col < row

    def group(hh):
        g = hh // HEADS_PER_GROUP
        return slice(g * LANES, (g + 1) * LANES)

    def own_lanes(hh):
        return (lane // HEAD_DIM) == (hh % HEADS_PER_GROUP)

    q_heads = []
    for hh in range(nh):
        qg = q_ref[0, :, group(hh)]
        q_heads.append(jnp.where(own_lanes(hh), qg, jnp.zeros_like(qg)))

    def scores(k, dst, diag):
        k0 = pl.multiple_of((qi - k) * t, t)
        for hh in range(nh):
            kt = kt_ref[0, group(hh), pl.ds(k0, t)]
            z = jnp.dot(q_heads[hh], kt, preferred_element_type=F32)
            if diag:
                z = jnp.where(before, z, MASKED_LOG)
            e = jnp.exp2(jnp.abs(z) * (-LOG2E))
            sp = jnp.maximum(z, 0.0) + jnp.log(1.0 + e)
            log_b = z - sp
            sp_ref[nh * dst + hh] = sp.astype(BF16)
            lb_ref[nh * dst + hh] = log_b
            cb_ref[nh * dst + hh] = c_ref[hh]
            c_ref[hh] += jnp.sum(sp, axis=1, keepdims=True)

    def weights(src, dst):
        for hh in range(nh):
            tail = jnp.dot(sp_ref[nh * src + hh], later, preferred_element_type=F32)
            logw = (lb_ref[nh * src + hh] - tail) - cb_ref[nh * src + hh]
            w_ref[nh * dst + hh] = jnp.exp(logw).astype(BF16)

    def values(k, src, valid=None):
        k0 = pl.multiple_of((qi - k) * t, t)
        for hh in range(nh):
            vv = v_ref[0, pl.ds(k0, t), group(hh)]
            if valid is not None:
                vv = jnp.where(valid, vv, jnp.zeros_like(vv))
            acc_ref[hh] += jnp.dot(w_ref[nh * src + hh], vv, preferred_element_type=F32)

    def saturated():
        return jnp.min(c_ref[...]) >= SATURATED

    acc_ref[...] = jnp.zeros_like(acc_ref)
    c_ref[...] = jnp.zeros_like(c_ref)

    second = jnp.minimum(qi, 1)
    scores(0, 0, True)
    scores(second, 1, False)
    weights(0, 0)
    values(0, 0)
    weights(1, 1)
    values(second, 1, qi >= 1)

    rest = qi - 1

    def step(it, new):
        old = 1 - new
        scores(jnp.minimum(2 + it, qi), new, False)
        weights(old, new)
        r = it - 2
        values(jnp.clip(2 + r, 2, qi), old, jnp.logical_and(r >= 0, r < rest))

    @pl.when(jnp.logical_and(rest > 0, jnp.logical_not(saturated())))
    def _():
        scores(2, 0, False)

        def cond(state):
            p, stop = state
            return jnp.logical_and(p < (rest + 2) // 2, stop == 0)

        def body(state):
            p, _ = state
            stop = saturated().astype(jnp.int32)
            step(2 * p + 1, 1)
            step(2 * p + 2, 0)
            return p + 1, stop

        lax.while_loop(cond, body, (jnp.int32(0), jnp.int32(0)))

    for g in range(nh // HEADS_PER_GROUP):
        out = acc_ref[HEADS_PER_GROUP * g]
        for j in range(1, HEADS_PER_GROUP):
            hh = HEADS_PER_GROUP * g + j
            out = jnp.where(own_lanes(hh), acc_ref[hh], out)
        o_ref[0, :, group(HEADS_PER_GROUP * g)] = out.astype(o_ref.dtype)


def _stick_breaking(q, kt, v):
    bsz, s, wb = q.shape
    t = SB_BLOCK
    nh = SB_HEADS
    width = nh * HEAD_DIM
    return pl.pallas_call(
        _sb_kernel,
        grid=(bsz, wb // width, s // t),
        in_specs=[
            pl.BlockSpec((1, t, width), lambda b, p, i: (b, i, p)),
            pl.BlockSpec((1, width, s), lambda b, p, i: (b, p, 0)),
            pl.BlockSpec((1, s, width), lambda b, p, i: (b, 0, p)),
        ],
        out_specs=pl.BlockSpec((1, t, width), lambda b, p, i: (b, i, p)),
        out_shape=jax.ShapeDtypeStruct((bsz, s, wb), BF16),
        scratch_shapes=[
            pltpu.VMEM((nh, t, LANES), F32),
            pltpu.VMEM((nh, t, 1), F32),
            pltpu.VMEM((2 * nh, t, 1), F32),
            pltpu.VMEM((2 * nh, t, t), BF16),
            pltpu.VMEM((2 * nh, t, t), F32),
            pltpu.VMEM((2 * nh, t, t), BF16),
        ],
        compiler_params=_params("parallel", "parallel", "parallel"),
        name="stick_breaking",
    )(q, kt, v)


HALO = 8


def _sc_kernel(x_ref, sh_ref, sc_ref, g_ref, ng_ref, win_ref, cw_ref, wout_ref, o_ref,
               cx_ref, z_ref):
    x = x_ref[0]
    tm, d = x.shape

    @pl.when(pl.program_id(1) == 0)
    def _():
        cx_ref[0:HALO, :] = jnp.zeros((HALO, d), F32)

    h = _modulated(x, ng_ref[0], sc_ref[0], sh_ref[0]).astype(BF16)
    cw = cw_ref[0]
    for c0 in range(0, d, FF_CHUNK):
        sl = slice(c0, c0 + FF_CHUNK)
        b_gate = jnp.dot(h, win_ref[:, sl], preferred_element_type=F32)
        c_gate = jnp.dot(h, win_ref[:, d + c0:d + c0 + FF_CHUNK], preferred_element_type=F32)
        xs = jnp.dot(h, win_ref[:, 2 * d + c0:2 * d + c0 + FF_CHUNK],
                     preferred_element_type=F32)
        cx_ref[HALO:HALO + tm, sl] = c_gate * xs
        y = (cw[2:3, sl] * cx_ref[HALO:HALO + tm, sl]
             + cw[1:2, sl] * cx_ref[HALO - 1:HALO - 1 + tm, sl]
             + cw[0:1, sl] * cx_ref[HALO - 2:HALO - 2 + tm, sl])
        z_ref[:, sl] = (b_gate * y).astype(BF16)
    cx_ref[0:HALO, :] = cx_ref[tm:tm + HALO, :]
    m = jnp.dot(z_ref[...], wout_ref[...], preferred_element_type=F32)
    o_ref[0] = x + g_ref[0] * m


def _short_conv(x, mod3, ng3, win, conv_w, wout, *, layer, sub, idx):
    bsz, s, d = x.shape
    tm = TOKEN_TILE
    return pl.pallas_call(
        _sc_kernel,
        grid=(bsz, s // tm),
        in_specs=[
            pl.BlockSpec((1, tm, d), lambda b, i: (b, i, 0)),
            _mod_spec(layer, 3 * sub + 0, d),
            _mod_spec(layer, 3 * sub + 1, d),
            _mod_spec(layer, 3 * sub + 2, d),
            pl.BlockSpec((1, 1, d), lambda b, i: (layer * 3 + sub, 0, 0)),
            _resident(win.shape, lambda b, i: (0, 0)),
            pl.BlockSpec((1,) + conv_w.shape[1:], lambda b, i: (idx, 0, 0)),
            _resident(wout.shape, lambda b, i: (0, 0)),
        ],
        out_specs=pl.BlockSpec((1, tm, d), lambda b, i: (b, i, 0)),
        out_shape=jax.ShapeDtypeStruct((bsz, s, d), F32),
        scratch_shapes=[pltpu.VMEM((tm + HALO, d), F32),
                        pltpu.VMEM((tm, d), BF16)],
        compiler_params=_params("parallel", "arbitrary"),
        name="short_conv",
    )(x, mod3, mod3, mod3, ng3, win, conv_w, wout)


def kernel(x, c, mod_w, mod_b, norm_g, ffn_w_gate, ffn_w_up, ffn_w_down, hy_w_in, hy_w_out,
           gm_vnorm_g, gm_w_s, gm_b_s, sc_w_in, sc_conv_w, sc_w_out, final_norm_g):
    depth = mod_w.shape[0]
    assert depth == 2, "layer 0's mixer hosts the weight casts of every later FFN"
    d = x.shape[-1]
    d_ff = ffn_w_gate.shape[-1]
    n_sub = norm_g.shape[1]

    mod3 = _modulation(c, mod_w, mod_b)
    ng3 = norm_g.reshape(depth * n_sub, 1, d)

    ffn_f32 = [w.reshape(-1, w.shape[-1]) for w in (ffn_w_gate, ffn_w_up, ffn_w_down)]
    ffn_rows = (d, d, d_ff)
    ffn_w = [(*[w[0, 0].astype(BF16) for w in (ffn_w_gate, ffn_w_up, ffn_w_down)], 0)]
    later_f32 = ([(w, r, r) for w, r in zip(ffn_f32, ffn_rows)]
                 + [(w, 2 * r, 2 * r) for w, r in zip(ffn_f32, ffn_rows)]
                 + [(w[0], 0, w.shape[1]) for w in (hy_w_out, sc_w_in, sc_w_out)])

    for layer in range(depth):
        i = layer // 2
        x = _ffn(x, mod3, ng3, ffn_w[2 * layer], final_norm_g, layer=layer, sub=0, final=False)
        if layer % 2 == 0:
            n_ha = gm_w_s.shape[1]
            wa_n = 2 * n_ha * HEAD_DIM
            wb = (hy_w_in.shape[2] - wa_n) // 3
            w_in = hy_w_in[i].astype(BF16)
            wkt = w_in[:, wa_n + wb:wa_n + 2 * wb].T
            head_of = jnp.arange(wa_n // 2) // HEAD_DIM
            gmat = (head_of[:, None] == head_of[None, :]).astype(BF16)
            ws2 = gm_w_s[i].astype(BF16).reshape(n_ha // 2, 2 * CHUNK, CHUNK)
            bs_tile = jnp.repeat(gm_b_s[i].T, HEAD_DIM, axis=1)
            ya, q, kt, v, *cast = _hy_in(x, mod3, ng3, w_in, wkt, gm_vnorm_g[i][None, :],
                                         gmat, ws2, bs_tile, later_f32, layer=layer, sub=1)
            ffn_w += [(*cast[0:3], 0), (*cast[3:6], 0), (*cast[3:6], 1)]
            hy_wo, sc_wi, sc_wo = cast[6:9]
            yb = _stick_breaking(q, kt, v)
            mix = (ya, yb, hy_wo)
        else:
            x = _short_conv(x, mod3, ng3, sc_wi, sc_conv_w, sc_wo, layer=layer, sub=1, idx=i)
            mix = None
        x = _ffn(x, mod3, ng3, ffn_w[2 * layer + 1], final_norm_g,
                 layer=layer, sub=2, final=(layer == depth - 1), mix=mix)
    return x
```

```python
import functools

import jax
import jax.numpy as jnp
from jax import lax
from jax.experimental import pallas as pl
from jax.experimental.pallas import tpu as pltpu

F32 = jnp.float32
BF16 = jnp.bfloat16

HEAD_DIM = 64
CHUNK = 128
EPS = 1e-6
MACARON_WEIGHT = 0.5
N_MOD = 9
MOD_ROWS = 8

LANES = 128
BF16_ROWS = 16
SB_BLOCK = 256
TOKEN_TILE = 1024
VMEM_LIMIT = 56 * 1024 * 1024


def _resident(block_shape, index_map):
    return pl.BlockSpec(block_shape, index_map, pipeline_mode=pl.Buffered(1))


def _params(*semantics):
    return pltpu.CompilerParams(dimension_semantics=semantics,
                                vmem_limit_bytes=VMEM_LIMIT)


def _rms(x):
    return lax.rsqrt(jnp.mean(x * x, axis=-1, keepdims=True) + EPS)


def _modulated(x, ng, sc, sh):
    return (x * _rms(x)) * (ng * (1.0 + sc)) + sh


def _silu(x):
    return x * jax.nn.sigmoid(x)


def _mod_kernel(c_ref, w_ref, b_ref, o_ref):
    cond = _silu(c_ref[...]).astype(BF16)
    res = jnp.dot(cond, w_ref[0].astype(BF16), preferred_element_type=F32) + b_ref[0]
    for b in range(MOD_ROWS):
        o_ref[b] = res[b:b + 1, :]


def _modulation(c, mod_w, mod_b):
    depth, d, n = mod_w.shape
    bsz = c.shape[0]
    assert n == N_MOD * d
    c8 = jnp.zeros((MOD_ROWS, d), F32).at[:bsz].set(c)
    return pl.pallas_call(
        _mod_kernel,
        grid=(depth, N_MOD),
        in_specs=[
            pl.BlockSpec((MOD_ROWS, d), lambda l, j: (0, 0)),
            pl.BlockSpec((1, d, d), lambda l, j: (l, 0, j)),
            pl.BlockSpec((1, 1, d), lambda l, j: (l, 0, j)),
        ],
        out_specs=pl.BlockSpec((MOD_ROWS, 1, d), lambda l, j: (l * N_MOD + j, 0, 0)),
        out_shape=jax.ShapeDtypeStruct((depth * N_MOD * MOD_ROWS, 1, d), F32),
        compiler_params=_params("parallel", "parallel"),
        name="adaln_mod",
    )(c8, mod_w, mod_b.reshape(depth, 1, n))


def _mod_spec(layer, k, d):
    base = (layer * N_MOD + k) * MOD_ROWS
    return pl.BlockSpec((1, 1, d), lambda b, i: (base + b, 0, 0))


FF_CHUNK = 256


def _ffn_kernel(*refs, final, mixed, f):
    if mixed:
        x_ref, ya_ref, yb_ref, wo_ref, gm_ref = refs[:5]
        refs = refs[5:]
    else:
        x_ref = refs[0]
        refs = refs[1:]
    (sh_ref, sc_ref, g_ref, ng_ref, wg_hbm, wu_hbm, wd_hbm, fg_ref, o_ref,
     a_ref, wg_ref, wu_ref, wd_ref, gu_stage, d_stage, gu_sem, d_sem) = refs
    d, d_ff = wg_ref.shape
    n_chunks = d_ff // FF_CHUNK
    cols = [slice(c * FF_CHUNK, (c + 1) * FF_CHUNK) for c in range(n_chunks)]

    def gu_copy(c, m):
        src = (wg_hbm, wu_hbm)[m].at[pl.ds(f * d, d), pl.ds(c * FF_CHUNK, FF_CHUNK)]
        return pltpu.make_async_copy(src, gu_stage.at[c % 2, m], gu_sem.at[c % 2, m])

    def d_copy(c):
        src = wd_hbm.at[pl.ds(f * d_ff + c * FF_CHUNK, FF_CHUNK), :]
        return pltpu.make_async_copy(src, d_stage.at[c % 2], d_sem.at[c % 2])

    def body(fetch):
        if fetch:
            for c in range(2):
                gu_copy(c, 0).start()
                gu_copy(c, 1).start()
                d_copy(c).start()
        if mixed:
            wa = ya_ref.shape[2]
            m = (jnp.dot(ya_ref[0], wo_ref[:wa, :], preferred_element_type=F32)
                 + jnp.dot(yb_ref[0], wo_ref[wa:, :], preferred_element_type=F32))
            x = x_ref[0] + gm_ref[0] * m
        else:
            x = x_ref[0]
        h = _modulated(x, ng_ref[0], sc_ref[0], sh_ref[0]).astype(BF16)
        for c in range(n_chunks):
            sl = cols[c]
            if fetch:
                gu_copy(c, 0).wait()
                gu_copy(c, 1).wait()
                wg_ref[:, sl] = gu_stage[c % 2, 0].astype(BF16)
                wu_ref[:, sl] = gu_stage[c % 2, 1].astype(BF16)
                if c + 2 < n_chunks:
                    gu_copy(c + 2, 0).start()
                    gu_copy(c + 2, 1).start()
            g = jnp.dot(h, wg_ref[:, sl], preferred_element_type=F32)
            u = jnp.dot(h, wu_ref[:, sl], preferred_element_type=F32)
            a_ref[:, sl] = (_silu(g) * u).astype(BF16)
        if fetch:
            for c in range(n_chunks):
                d_copy(c).wait()
                wd_ref[cols[c], :] = d_stage[c % 2].astype(BF16)
                if c + 2 < n_chunks:
                    d_copy(c + 2).start()
        y = jnp.dot(a_ref[...], wd_ref[...], preferred_element_type=F32)
        out = x + (MACARON_WEIGHT * g_ref[0]) * y
        if final:
            out = (out * _rms(out)) * fg_ref[...]
        o_ref[0] = out

    first = jnp.logical_and(pl.program_id(0) == 0, pl.program_id(1) == 0)
    pl.when(first)(lambda: body(True))
    pl.when(jnp.logical_not(first))(lambda: body(False))


def _ffn(x, mod3, ng3, weights, final_g, *, layer, sub, final, mix=None):
    bsz, s, d = x.shape
    wg, wu, wd, f = weights
    d_ff = wg.shape[-1]
    tm = TOKEN_TILE if mix is None else TOKEN_TILE // 2
    tile = lambda b, i: (b, i, 0)
    hbm = pl.BlockSpec(memory_space=pl.ANY)
    mix_specs, mix_args = [], []
    if mix is not None:
        ya, yb, wo = mix
        mix_specs = [
            pl.BlockSpec((1, tm, ya.shape[2]), tile),
            pl.BlockSpec((1, tm, yb.shape[2]), tile),
            _resident(wo.shape, lambda b, i: (0, 0)),
            _mod_spec(layer, 3 * (sub - 1) + 2, d),
        ]
        mix_args = [ya, yb, wo, mod3]
    return pl.pallas_call(
        functools.partial(_ffn_kernel, final=final, mixed=mix is not None, f=f),
        grid=(bsz, s // tm),
        in_specs=[
            pl.BlockSpec((1, tm, d), tile),
            *mix_specs,
            _mod_spec(layer, 3 * sub + 0, d),
            _mod_spec(layer, 3 * sub + 1, d),
            _mod_spec(layer, 3 * sub + 2, d),
            pl.BlockSpec((1, 1, d), lambda b, i: (layer * 3 + sub, 0, 0)),
            hbm, hbm, hbm,
            pl.BlockSpec((1, d), lambda b, i: (0, 0)),
        ],
        out_specs=pl.BlockSpec((1, tm, d), tile),
        out_shape=jax.ShapeDtypeStruct((bsz, s, d), F32),
        scratch_shapes=[
            pltpu.VMEM((tm, d_ff), BF16),
            pltpu.VMEM((d, d_ff), BF16),
            pltpu.VMEM((d, d_ff), BF16),
            pltpu.VMEM((d_ff, d), BF16),
            pltpu.VMEM((2, 2, d, FF_CHUNK), F32),
            pltpu.VMEM((2, FF_CHUNK, d), F32),
            pltpu.SemaphoreType.DMA((2, 2)),
            pltpu.SemaphoreType.DMA((2,)),
        ],
        compiler_params=_params("arbitrary", "arbitrary"),
        name=f"ffn_l{layer}_s{sub}",
    )(x, *mix_args, mod3, mod3, mod3, ng3, wg, wu, wd, final_g.reshape(1, d))


def _hyin_kernel(*refs, n_cast):
    (x_ref, sh_ref, sc_ref, ng_ref, wa_ref, wq_ref, wkt_ref, wv_ref, vg_ref,
     gmat_ref, ws_ref, bs_ref) = refs[:12]
    cast_in = refs[12:12 + n_cast]
    ya_ref, q_ref, kt_ref, v_ref = refs[12 + n_cast:16 + n_cast]
    cast_out = refs[16 + n_cast:]
    for src, dst in zip(cast_in, cast_out):
        dst[...] = src[...].astype(BF16)

    x = x_ref[0]
    tm = x.shape[0]
    h = _modulated(x, ng_ref[0], sc_ref[0], sh_ref[0]).astype(BF16)

    uv = jnp.dot(h, wa_ref[...], preferred_element_type=F32)
    q = jnp.dot(h, wq_ref[...], preferred_element_type=F32)
    q_ref[0] = (q * (HEAD_DIM ** -0.5)).astype(BF16)
    uv = 0.5 * uv * (1.0 + lax.erf(uv * (2.0 ** -0.5)))
    wa = uv.shape[1] // 2
    u, v = uv[:, :wa], uv[:, wa:]
    ssq = jnp.dot((v * v).astype(BF16), gmat_ref[...], preferred_element_type=F32)
    v_ref[0] = jnp.dot(h, wv_ref[...], preferred_element_type=F32).astype(BF16)
    vn = (v * lax.rsqrt(ssq * (1.0 / HEAD_DIM) + EPS) * vg_ref[...]).astype(BF16)

    n_pairs = wa // LANES
    r2 = lax.broadcasted_iota(jnp.int32, (2 * CHUNK, CHUNK), 0)
    c2 = lax.broadcasted_iota(jnp.int32, (2 * CHUNK, CHUNK), 1)
    causal = (r2 & (CHUNK - 1)) >= c2
    first_head = lax.broadcasted_iota(jnp.int32, (CHUNK, LANES), 1) < HEAD_DIM
    w_pairs = [jnp.where(causal, ws_ref[p], jnp.zeros_like(ws_ref[p])) for p in range(n_pairs)]
    for ci in range(0, tm // CHUNK, 2):
        rows = [slice((ci + j) * CHUNK, (ci + j + 1) * CHUNK) for j in range(2)]
        parts = [[], []]
        for p in range(n_pairs):
            lanes = slice(p * LANES, (p + 1) * LANES)
            vp = jnp.concatenate([vn[rows[0], lanes], vn[rows[1], lanes]], axis=1)
            r = jnp.dot(w_pairs[p], vp, preferred_element_type=F32)
            for j in range(2):
                rj = r[:, j * LANES:(j + 1) * LANES]
                parts[j].append(jnp.where(first_head, rj[:CHUNK], rj[CHUNK:]))
        for j in range(2):
            sv = jnp.concatenate(parts[j], axis=1) + bs_ref[...]
            ya_ref[0, rows[j], :] = (u[rows[j]] * sv).astype(BF16)
    kt_ref[0] = lax.dot_general(wkt_ref[...], h, (((1,), (1,)), ((), ())),
                                preferred_element_type=F32).astype(BF16)


def _hy_in(x, mod3, ng3, w_in, wkt, vg, gmat, ws2, bs_tile, cast, *, layer, sub):
    bsz, s, d = x.shape
    tm = TOKEN_TILE
    n_i = s // tm
    wb = wkt.shape[0]
    wa_n = w_in.shape[1] - 3 * wb
    assert wa_n % wb == 0
    const2 = lambda b, i: (0, 0)

    def cast_specs(w, start, count):
        rb = count // (bsz * n_i)
        assert rb * bsz * n_i == count and rb % BF16_ROWS == 0 and start % rb == 0
        cols = w.shape[1]
        return (pl.BlockSpec((rb, cols), lambda b, i: (start // rb + b * n_i + i, 0)),
                pl.BlockSpec((rb, cols), lambda b, i: (b * n_i + i, 0)),
                jax.ShapeDtypeStruct((count, cols), BF16))

    casts = [cast_specs(*c) for c in cast]
    return pl.pallas_call(
        functools.partial(_hyin_kernel, n_cast=len(casts)),
        grid=(bsz, n_i),
        in_specs=[
            pl.BlockSpec((1, tm, d), lambda b, i: (b, i, 0)),
            _mod_spec(layer, 3 * sub + 0, d),
            _mod_spec(layer, 3 * sub + 1, d),
            pl.BlockSpec((1, 1, d), lambda b, i: (layer * 3 + sub, 0, 0)),
            _resident((d, wa_n), const2),
            _resident((d, wb), lambda b, i: (0, wa_n // wb)),
            _resident(wkt.shape, const2),
            _resident((d, wb), lambda b, i: (0, wa_n // wb + 2)),
            pl.BlockSpec(vg.shape, const2),
            _resident(gmat.shape, const2),
            _resident(ws2.shape, lambda b, i: (0, 0, 0)),
            pl.BlockSpec(bs_tile.shape, const2),
            *[c[0] for c in casts],
        ],
        out_specs=[
            pl.BlockSpec((1, tm, wa_n // 2), lambda b, i: (b, i, 0)),
            pl.BlockSpec((1, tm, wb), lambda b, i: (b, i, 0)),
            pl.BlockSpec((1, wb, tm), lambda b, i: (b, 0, i)),
            pl.BlockSpec((1, tm, wb), lambda b, i: (b, i, 0)),
            *[c[1] for c in casts],
        ],
        out_shape=[
            jax.ShapeDtypeStruct((bsz, s, wa_n // 2), BF16),
            jax.ShapeDtypeStruct((bsz, s, wb), BF16),
            jax.ShapeDtypeStruct((bsz, wb, s), BF16),
            jax.ShapeDtypeStruct((bsz, s, wb), BF16),
            *[c[2] for c in casts],
        ],
        compiler_params=_params("parallel", "parallel"),
        name="hy_in",
    )(x, mod3, mod3, ng3, w_in, w_in, wkt, w_in, vg, gmat, ws2, bs_tile, *[c[0] for c in cast])


MASKED_LOG = -1e30
LOG2E = 1.4426950408889634
SATURATED = 105.0
SB_HEADS = 4
HEADS_PER_GROUP = LANES // HEAD_DIM


def _sb_kernel(q_ref, kt_ref, v_ref, o_ref,
               acc_ref, c_ref, cb_ref, sp_ref, lb_ref, w_ref):
    t = q_ref.shape[1]
    nh = acc_ref.shape[0]
    qi = pl.program_id(2)
    lane = lax.broadcasted_iota(jnp.int32, (t, LANES), 1)
    row = lax.broadcasted_iota(jnp.int32, (t, t), 0)
    col = lax.broadcasted_iota(jnp.int32, (t, t), 1)
    later = (row > col).astype(BF16)
    before = col < row

    def group(hh):
        g = hh // HEADS_PER_GROUP
        return slice(g * LANES, (g + 1) * LANES)

    def own_lanes(hh):
        return (lane // HEAD_DIM) == (hh % HEADS_PER_GROUP)

    q_heads = []
    for hh in range(nh):
        qg = q_ref[0, :, group(hh)]
        q_heads.append(jnp.where(own_lanes(hh), qg, jnp.zeros_like(qg)))

    def scores(k, dst, diag):
        k0 = pl.multiple_of((qi - k) * t, t)
        for hh in range(nh):
            kt = kt_ref[0, group(hh), pl.ds(k0, t)]
            z = jnp.dot(q_heads[hh], kt, preferred_element_type=F32)
            if diag:
                z = jnp.where(before, z, MASKED_LOG)
            e = jnp.exp2(jnp.abs(z) * (-LOG2E))
            sp = jnp.maximum(z, 0.0) + jnp.log(1.0 + e)
            log_b = z - sp
            sp_ref[nh * dst + hh] = sp.astype(BF16)
            lb_ref[nh * dst + hh] = log_b
            cb_ref[nh * dst + hh] = c_ref[hh]
            c_ref[hh] += jnp.sum(sp, axis=1, keepdims=True)

    def weights(src, dst):
        for hh in range(nh):
            tail = jnp.dot(sp_ref[nh * src + hh], later, preferred_element_type=F32)
            logw = (lb_ref[nh * src + hh] - tail) - cb_ref[nh * src + hh]
            w_ref[nh * dst + hh] = jnp.exp(logw).astype(BF16)

    def values(k, src, valid=None):
        k0 = pl.multiple_of((qi - k) * t, t)
        for hh in range(nh):
            vv = v_ref[0, pl.ds(k0, t), group(hh)]
            if valid is not None:
                vv = jnp.where(valid, vv, jnp.zeros_like(vv))
            acc_ref[hh] += jnp.dot(w_ref[nh * src + hh], vv, preferred_element_type=F32)

    def saturated():
        return jnp.min(c_ref[...]) >= SATURATED

    acc_ref[...] = jnp.zeros_like(acc_ref)
    c_ref[...] = jnp.zeros_like(c_ref)

    second = jnp.minimum(qi, 1)
    scores(0, 0, True)
    scores(second, 1, False)
    weights(0, 0)
    values(0, 0)
    weights(1, 1)
    values(second, 1, qi >= 1)

    rest = qi - 1

    def step(it, new):
        old = 1 - new
        scores(jnp.minimum(2 + it, qi), new, False)
        weights(old, new)
        r = it - 2
        values(jnp.clip(2 + r, 2, qi), old, jnp.logical_and(r >= 0, r < rest))

    @pl.when(jnp.logical_and(rest > 0, jnp.logical_not(saturated())))
    def _():
        scores(2, 0, False)

        def cond(state):
            p, stop = state
            return jnp.logical_and(p < (rest + 2) // 2, stop == 0)

        def body(state):
            p, _ = state
            stop = saturated().astype(jnp.int32)
            step(2 * p + 1, 1)
            step(2 * p + 2, 0)
            return p + 1, stop

        lax.while_loop(cond, body, (jnp.int32(0), jnp.int32(0)))

    for g in range(nh // HEADS_PER_GROUP):
        out = acc_ref[HEADS_PER_GROUP * g]
        for j in range(1, HEADS_PER_GROUP):
            hh = HEADS_PER_GROUP * g + j
            out = jnp.where(own_lanes(hh), acc_ref[hh], out)
        o_ref[0, :, group(HEADS_PER_GROUP * g)] = out.astype(o_ref.dtype)


def _stick_breaking(q, kt, v):
    bsz, s, wb = q.shape
    t = SB_BLOCK
    nh = SB_HEADS
    width = nh * HEAD_DIM
    return pl.pallas_call(
        _sb_kernel,
        grid=(bsz, wb // width, s // t),
        in_specs=[
            pl.BlockSpec((1, t, width), lambda b, p, i: (b, i, p)),
            pl.BlockSpec((1, width, s), lambda b, p, i: (b, p, 0)),
            pl.BlockSpec((1, s, width), lambda b, p, i: (b, 0, p)),
        ],
        out_specs=pl.BlockSpec((1, t, width), lambda b, p, i: (b, i, p)),
        out_shape=jax.ShapeDtypeStruct((bsz, s, wb), BF16),
        scratch_shapes=[
            pltpu.VMEM((nh, t, LANES), F32),
            pltpu.VMEM((nh, t, 1), F32),
            pltpu.VMEM((2 * nh, t, 1), F32),
            pltpu.VMEM((2 * nh, t, t), BF16),
            pltpu.VMEM((2 * nh, t, t), F32),
            pltpu.VMEM((2 * nh, t, t), BF16),
        ],
        compiler_params=_params("parallel", "parallel", "parallel"),
        name="stick_breaking",
    )(q, kt, v)


HALO = 8


def _sc_kernel(x_ref, sh_ref, sc_ref, g_ref, ng_ref, win_ref, cw_ref, wout_ref, o_ref,
               cx_ref, z_ref):
    x = x_ref[0]
    tm, d = x.shape

    @pl.when(pl.program_id(1) == 0)
    def _():
        cx_ref[0:HALO, :] = jnp.zeros((HALO, d), F32)

    h = _modulated(x, ng_ref[0], sc_ref[0], sh_ref[0]).astype(BF16)
    cw = cw_ref[0]
    for c0 in range(0, d, FF_CHUNK):
        sl = slice(c0, c0 + FF_CHUNK)
        b_gate = jnp.dot(h, win_ref[:, sl], preferred_element_type=F32)
        c_gate = jnp.dot(h, win_ref[:, d + c0:d + c0 + FF_CHUNK], preferred_element_type=F32)
        xs = jnp.dot(h, win_ref[:, 2 * d + c0:2 * d + c0 + FF_CHUNK],
                     preferred_element_type=F32)
        cx_ref[HALO:HALO + tm, sl] = c_gate * xs
        y = (cw[2:3, sl] * cx_ref[HALO:HALO + tm, sl]
             + cw[1:2, sl] * cx_ref[HALO - 1:HALO - 1 + tm, sl]
             + cw[0:1, sl] * cx_ref[HALO - 2:HALO - 2 + tm, sl])
        z_ref[:, sl] = (b_gate * y).astype(BF16)
    cx_ref[0:HALO, :] = cx_ref[tm:tm + HALO, :]
    m = jnp.dot(z_ref[...], wout_ref[...], preferred_element_type=F32)
    o_ref[0] = x + g_ref[0] * m


def _short_conv(x, mod3, ng3, win, conv_w, wout, *, layer, sub, idx):
    bsz, s, d = x.shape
    tm = TOKEN_TILE
    return pl.pallas_call(
        _sc_kernel,
        grid=(bsz, s // tm),
        in_specs=[
            pl.BlockSpec((1, tm, d), lambda b, i: (b, i, 0)),
            _mod_spec(layer, 3 * sub + 0, d),
            _mod_spec(layer, 3 * sub + 1, d),
            _mod_spec(layer, 3 * sub + 2, d),
            pl.BlockSpec((1, 1, d), lambda b, i: (layer * 3 + sub, 0, 0)),
            _resident(win.shape, lambda b, i: (0, 0)),
            pl.BlockSpec((1,) + conv_w.shape[1:], lambda b, i: (idx, 0, 0)),
            _resident(wout.shape, lambda b, i: (0, 0)),
        ],
        out_specs=pl.BlockSpec((1, tm, d), lambda b, i: (b, i, 0)),
        out_shape=jax.ShapeDtypeStruct((bsz, s, d), F32),
        scratch_shapes=[pltpu.VMEM((tm + HALO, d), F32),
                        pltpu.VMEM((tm, d), BF16)],
        compiler_params=_params("parallel", "arbitrary"),
        name="short_conv",
    )(x, mod3, mod3, mod3, ng3, win, conv_w, wout)


def kernel(x, c, mod_w, mod_b, norm_g, ffn_w_gate, ffn_w_up, ffn_w_down, hy_w_in, hy_w_out,
           gm_vnorm_g, gm_w_s, gm_b_s, sc_w_in, sc_conv_w, sc_w_out, final_norm_g):
    depth = mod_w.shape[0]
    assert depth == 2, "layer 0's mixer input kernel hosts the weight casts of layer 1's mixer"
    d = x.shape[-1]
    n_sub = norm_g.shape[1]

    mod3 = _modulation(c, mod_w, mod_b)
    ng3 = norm_g.reshape(depth * n_sub, 1, d)

    ffn_f32 = [w.reshape(-1, w.shape[-1]) for w in (ffn_w_gate, ffn_w_up, ffn_w_down)]
    later_f32 = [(w[0], 0, w.shape[1]) for w in (hy_w_out, sc_w_in, sc_w_out)]

    for layer in range(depth):
        i = layer // 2
        x = _ffn(x, mod3, ng3, (*ffn_f32, 2 * layer), final_norm_g,
                 layer=layer, sub=0, final=False)
        if layer % 2 == 0:
            n_ha = gm_w_s.shape[1]
            wa_n = 2 * n_ha * HEAD_DIM
            wb = (hy_w_in.shape[2] - wa_n) // 3
            w_in = hy_w_in[i].astype(BF16)
            wkt = w_in[:, wa_n + wb:wa_n + 2 * wb].T
            head_of = jnp.arange(wa_n // 2) // HEAD_DIM
            gmat = (head_of[:, None] == head_of[None, :]).astype(BF16)
            ws2 = gm_w_s[i].astype(BF16).reshape(n_ha // 2, 2 * CHUNK, CHUNK)
            bs_tile = jnp.repeat(gm_b_s[i].T, HEAD_DIM, axis=1)
            ya, q, kt, v, hy_wo, sc_wi, sc_wo = _hy_in(
                x, mod3, ng3, w_in, wkt, gm_vnorm_g[i][None, :], gmat, ws2, bs_tile, later_f32,
                layer=layer, sub=1)
            yb = _stick_breaking(q, kt, v)
            mix = (ya, yb, hy_wo)
        else:
            x = _short_conv(x, mod3, ng3, sc_wi, sc_conv_w, sc_wo, layer=layer, sub=1, idx=i)
            mix = None
        x = _ffn(x, mod3, ng3, (*ffn_f32, 2 * layer + 1), final_norm_g,
                 layer=layer, sub=2, final=(layer == depth - 1), mix=mix)
    return x
```

```python
import functools

import jax
import jax.numpy as jnp
from jax import lax
from jax.experimental import pallas as pl
from jax.experimental.pallas import tpu as pltpu

F32 = jnp.float32
BF16 = jnp.bfloat16

HEAD_DIM = 64
CHUNK = 128
EPS = 1e-6
MACARON_WEIGHT = 0.5
N_MOD = 9
MOD_ROWS = 8

LANES = 128
BF16_ROWS = 16
SB_BLOCK = 256
TOKEN_TILE = 1024
VMEM_LIMIT = 56 * 1024 * 1024


def _resident(block_shape, index_map):
    return pl.BlockSpec(block_shape, index_map, pipeline_mode=pl.Buffered(1))


def _params(*semantics):
    return pltpu.CompilerParams(dimension_semantics=semantics,
                                vmem_limit_bytes=VMEM_LIMIT)


def _rms(x):
    return lax.rsqrt(jnp.mean(x * x, axis=-1, keepdims=True) + EPS)


def _modulated(x, ng, sc, sh):
    return (x * _rms(x)) * (ng * (1.0 + sc)) + sh


def _silu(x):
    return x * jax.nn.sigmoid(x)


def _mod_kernel(c_ref, w_ref, b_ref, o_ref):
    cond = _silu(c_ref[...]).astype(BF16)
    res = jnp.dot(cond, w_ref[0].astype(BF16), preferred_element_type=F32) + b_ref[0]
    for b in range(MOD_ROWS):
        o_ref[b] = res[b:b + 1, :]


def _modulation(c, mod_w, mod_b):
    depth, d, n = mod_w.shape
    bsz = c.shape[0]
    assert n == N_MOD * d
    c8 = jnp.zeros((MOD_ROWS, d), F32).at[:bsz].set(c)
    return pl.pallas_call(
        _mod_kernel,
        grid=(depth, N_MOD),
        in_specs=[
            pl.BlockSpec((MOD_ROWS, d), lambda l, j: (0, 0)),
            pl.BlockSpec((1, d, d), lambda l, j: (l, 0, j)),
            pl.BlockSpec((1, 1, d), lambda l, j: (l, 0, j)),
        ],
        out_specs=pl.BlockSpec((MOD_ROWS, 1, d), lambda l, j: (l * N_MOD + j, 0, 0)),
        out_shape=jax.ShapeDtypeStruct((depth * N_MOD * MOD_ROWS, 1, d), F32),
        compiler_params=_params("parallel", "parallel"),
        name="adaln_mod",
    )(c8, mod_w, mod_b.reshape(depth, 1, n))


def _mod_spec(layer, k, d):
    base = (layer * N_MOD + k) * MOD_ROWS
    return pl.BlockSpec((1, 1, d), lambda b, i: (base + b, 0, 0))


FF_CHUNK = 256


def _ffn_kernel(*refs, final, mixed):
    if mixed:
        x_ref, ya_ref, yb_ref, wo_ref, gm_ref = refs[:5]
        wa = ya_ref.shape[2]
        m = (jnp.dot(ya_ref[0], wo_ref[:wa, :], preferred_element_type=F32)
             + jnp.dot(yb_ref[0], wo_ref[wa:, :], preferred_element_type=F32))
        x = x_ref[0] + gm_ref[0] * m
        refs = refs[5:]
    else:
        x = refs[0][0]
        refs = refs[1:]
    sh_ref, sc_ref, g_ref, ng_ref, wg_ref, wu_ref, wd_ref, fg_ref, o_ref, a_ref = refs
    h = _modulated(x, ng_ref[0], sc_ref[0], sh_ref[0]).astype(BF16)
    d_ff = a_ref.shape[1]
    for c0 in range(0, d_ff, FF_CHUNK):
        sl = slice(c0, c0 + FF_CHUNK)
        g = jnp.dot(h, wg_ref[:, sl], preferred_element_type=F32)
        u = jnp.dot(h, wu_ref[:, sl], preferred_element_type=F32)
        a_ref[:, sl] = (_silu(g) * u).astype(BF16)
    y = jnp.dot(a_ref[...], wd_ref[...], preferred_element_type=F32)
    out = x + (MACARON_WEIGHT * g_ref[0]) * y
    if final:
        out = (out * _rms(out)) * fg_ref[...]
    o_ref[0] = out


def _ffn(x, mod3, ng3, weights, final_g, *, layer, sub, final, mix=None):
    bsz, s, d = x.shape
    wg, wu, wd, f = weights
    d_ff = wg.shape[-1]
    tm = TOKEN_TILE
    tile = lambda b, i: (b, i, 0)
    mix_specs, mix_args = [], []
    if mix is not None:
        ya, yb, wo = mix
        mix_specs = [
            pl.BlockSpec((1, tm, ya.shape[2]), tile),
            pl.BlockSpec((1, tm, yb.shape[2]), tile),
            _resident(wo.shape, lambda b, i: (0, 0)),
            _mod_spec(layer, 3 * (sub - 1) + 2, d),
        ]
        mix_args = [ya, yb, wo, mod3]
    return pl.pallas_call(
        functools.partial(_ffn_kernel, final=final, mixed=mix is not None),
        grid=(bsz, s // tm),
        in_specs=[
            pl.BlockSpec((1, tm, d), tile),
            *mix_specs,
            _mod_spec(layer, 3 * sub + 0, d),
            _mod_spec(layer, 3 * sub + 1, d),
            _mod_spec(layer, 3 * sub + 2, d),
            pl.BlockSpec((1, 1, d), lambda b, i: (layer * 3 + sub, 0, 0)),
            _resident((d, d_ff), lambda b, i: (f, 0)),
            _resident((d, d_ff), lambda b, i: (f, 0)),
            _resident((d_ff, d), lambda b, i: (f, 0)),
            pl.BlockSpec((1, d), lambda b, i: (0, 0)),
        ],
        out_specs=pl.BlockSpec((1, tm, d), tile),
        out_shape=jax.ShapeDtypeStruct((bsz, s, d), F32),
        scratch_shapes=[pltpu.VMEM((tm, d_ff), BF16)],
        compiler_params=_params("parallel", "parallel"),
        name=f"ffn_l{layer}_s{sub}",
    )(x, *mix_args, mod3, mod3, mod3, ng3, wg, wu, wd, final_g.reshape(1, d))


def _hyin_kernel(*refs, n_cast):
    (x_ref, sh_ref, sc_ref, ng_ref, wa_ref, wq_ref, wkt_ref, wv_ref, vg_ref,
     gmat_ref, ws_ref, bs_ref) = refs[:12]
    cast_in = refs[12:12 + n_cast]
    ya_ref, q_ref, kt_ref, v_ref = refs[12 + n_cast:16 + n_cast]
    cast_out = refs[16 + n_cast:]
    for src, dst in zip(cast_in, cast_out):
        dst[...] = src[...].astype(BF16)

    x = x_ref[0]
    tm = x.shape[0]
    h = _modulated(x, ng_ref[0], sc_ref[0], sh_ref[0]).astype(BF16)

    uv = jnp.dot(h, wa_ref[...], preferred_element_type=F32)
    q = jnp.dot(h, wq_ref[...], preferred_element_type=F32)
    q_ref[0] = (q * (HEAD_DIM ** -0.5)).astype(BF16)
    uv = 0.5 * uv * (1.0 + lax.erf(uv * (2.0 ** -0.5)))
    wa = uv.shape[1] // 2
    u, v = uv[:, :wa], uv[:, wa:]
    ssq = jnp.dot((v * v).astype(BF16), gmat_ref[...], preferred_element_type=F32)
    v_ref[0] = jnp.dot(h, wv_ref[...], preferred_element_type=F32).astype(BF16)
    vn = (v * lax.rsqrt(ssq * (1.0 / HEAD_DIM) + EPS) * vg_ref[...]).astype(BF16)

    n_pairs = wa // LANES
    r2 = lax.broadcasted_iota(jnp.int32, (2 * CHUNK, CHUNK), 0)
    c2 = lax.broadcasted_iota(jnp.int32, (2 * CHUNK, CHUNK), 1)
    causal = (r2 & (CHUNK - 1)) >= c2
    first_head = lax.broadcasted_iota(jnp.int32, (CHUNK, LANES), 1) < HEAD_DIM
    w_pairs = [jnp.where(causal, ws_ref[p], jnp.zeros_like(ws_ref[p])) for p in range(n_pairs)]
    for ci in range(0, tm // CHUNK, 2):
        rows = [slice((ci + j) * CHUNK, (ci + j + 1) * CHUNK) for j in range(2)]
        parts = [[], []]
        for p in range(n_pairs):
            lanes = slice(p * LANES, (p + 1) * LANES)
            vp = jnp.concatenate([vn[rows[0], lanes], vn[rows[1], lanes]], axis=1)
            r = jnp.dot(w_pairs[p], vp, preferred_element_type=F32)
            for j in range(2):
                rj = r[:, j * LANES:(j + 1) * LANES]
                parts[j].append(jnp.where(first_head, rj[:CHUNK], rj[CHUNK:]))
        for j in range(2):
            sv = jnp.concatenate(parts[j], axis=1) + bs_ref[...]
            ya_ref[0, rows[j], :] = (u[rows[j]] * sv).astype(BF16)
    kt_ref[0] = lax.dot_general(wkt_ref[...], h, (((1,), (1,)), ((), ())),
                                preferred_element_type=F32).astype(BF16)


def _hy_in(x, mod3, ng3, w_in, wkt, vg, gmat, ws2, bs_tile, cast, *, layer, sub):
    bsz, s, d = x.shape
    tm = TOKEN_TILE
    n_i = s // tm
    wb = wkt.shape[0]
    wa_n = w_in.shape[1] - 3 * wb
    assert wa_n % wb == 0
    const2 = lambda b, i: (0, 0)

    def cast_specs(w, start, count):
        rb = count // (bsz * n_i)
        assert rb * bsz * n_i == count and rb % BF16_ROWS == 0 and start % rb == 0
        cols = w.shape[1]
        return (pl.BlockSpec((rb, cols), lambda b, i: (start // rb + b * n_i + i, 0)),
                pl.BlockSpec((rb, cols), lambda b, i: (b * n_i + i, 0)),
                jax.ShapeDtypeStruct((count, cols), BF16))

    casts = [cast_specs(*c) for c in cast]
    return pl.pallas_call(
        functools.partial(_hyin_kernel, n_cast=len(casts)),
        grid=(bsz, n_i),
        in_specs=[
            pl.BlockSpec((1, tm, d), lambda b, i: (b, i, 0)),
            _mod_spec(layer, 3 * sub + 0, d),
            _mod_spec(layer, 3 * sub + 1, d),
            pl.BlockSpec((1, 1, d), lambda b, i: (layer * 3 + sub, 0, 0)),
            _resident((d, wa_n), const2),
            _resident((d, wb), lambda b, i: (0, wa_n // wb)),
            _resident(wkt.shape, const2),
            _resident((d, wb), lambda b, i: (0, wa_n // wb + 2)),
            pl.BlockSpec(vg.shape, const2),
            _resident(gmat.shape, const2),
            _resident(ws2.shape, lambda b, i: (0, 0, 0)),
            pl.BlockSpec(bs_tile.shape, const2),
            *[c[0] for c in casts],
        ],
        out_specs=[
            pl.BlockSpec((1, tm, wa_n // 2), lambda b, i: (b, i, 0)),
            pl.BlockSpec((1, tm, wb), lambda b, i: (b, i, 0)),
            pl.BlockSpec((1, wb, tm), lambda b, i: (b, 0, i)),
            pl.BlockSpec((1, tm, wb), lambda b, i: (b, i, 0)),
            *[c[1] for c in casts],
        ],
        out_shape=[
            jax.ShapeDtypeStruct((bsz, s, wa_n // 2), BF16),
            jax.ShapeDtypeStruct((bsz, s, wb), BF16),
            jax.ShapeDtypeStruct((bsz, wb, s), BF16),
            jax.ShapeDtypeStruct((bsz, s, wb), BF16),
            *[c[2] for c in casts],
        ],
        compiler_params=_params("parallel", "parallel"),
        name="hy_in",
    )(x, mod3, mod3, ng3, w_in, w_in, wkt, w_in, vg, gmat, ws2, bs_tile, *[c[0] for c in cast])


MASKED_LOG = -1e30
LOG2E = 1.4426950408889634
SATURATED = 105.0
SB_HEADS = 4
HEADS_PER_GROUP = LANES // HEAD_DIM


def _sb_kernel(q_ref, kt_ref, v_ref, o_ref,
               acc_ref, c_ref, cb_ref, sp_ref, lb_ref, w_ref):
    t = q_ref.shape[1]
    nh = acc_ref.shape[0]
    qi = pl.program_id(2)
    lane = lax.broadcasted_iota(jnp.int32, (t, LANES), 1)
    row = lax.broadcasted_iota(jnp.int32, (t, t), 0)
    col = lax.broadcasted_iota(jnp.int32, (t, t), 1)
    later = (row > col).astype(BF16)
    before = col < row
    ones = jnp.ones((t, LANES), BF16)

    def group(hh):
        g = hh // HEADS_PER_GROUP
        return slice(g * LANES, (g + 1) * LANES)

    def own_lanes(hh):
        return (lane // HEAD_DIM) == (hh % HEADS_PER_GROUP)

    q_heads = []
    for hh in range(nh):
        qg = q_ref[0, :, group(hh)]
        q_heads.append(jnp.where(own_lanes(hh), qg, jnp.zeros_like(qg)))

    def scores(k, dst, diag):
        k0 = pl.multiple_of((qi - k) * t, t)
        for hh in range(nh):
            kt = kt_ref[0, group(hh), pl.ds(k0, t)]
            z = jnp.dot(q_heads[hh], kt, preferred_element_type=F32)
            if diag:
                z = jnp.where(before, z, MASKED_LOG)
            e = jnp.exp2(jnp.abs(z) * (-LOG2E))
            sp = jnp.maximum(z, 0.0) + jnp.log(1.0 + e)
            log_b = z - sp
            sp_ref[nh * dst + hh] = sp.astype(BF16)
            lb_ref[nh * dst + hh] = log_b
            cb_ref[nh * dst + hh] = c_ref[hh]
            c_ref[hh] += jnp.dot(sp.astype(BF16), ones, preferred_element_type=F32)

    def weights(src, dst):
        for hh in range(nh):
            tail = jnp.dot(sp_ref[nh * src + hh], later, preferred_element_type=F32)
            cb = cb_ref[nh * src + hh]
            logw = (lb_ref[nh * src + hh] - tail) - jnp.concatenate([cb] * (t // LANES), axis=1)
            w_ref[nh * dst + hh] = jnp.exp(logw).astype(BF16)

    def values(k, src, valid=None):
        k0 = pl.multiple_of((qi - k) * t, t)
        for hh in range(nh):
            vv = v_ref[0, pl.ds(k0, t), group(hh)]
            if valid is not None:
                vv = jnp.where(valid, vv, jnp.zeros_like(vv))
            acc_ref[hh] += jnp.dot(w_ref[nh * src + hh], vv, preferred_element_type=F32)

    def saturated():
        return jnp.min(c_ref[...]) >= SATURATED

    acc_ref[...] = jnp.zeros_like(acc_ref)
    c_ref[...] = jnp.zeros_like(c_ref)

    second = jnp.minimum(qi, 1)
    scores(0, 0, True)
    scores(second, 1, False)
    weights(0, 0)
    values(0, 0)
    weights(1, 1)
    values(second, 1, qi >= 1)

    rest = qi - 1

    def step(it, new):
        old = 1 - new
        scores(jnp.minimum(2 + it, qi), new, False)
        weights(old, new)
        r = it - 2
        values(jnp.clip(2 + r, 2, qi), old, jnp.logical_and(r >= 0, r < rest))

    @pl.when(jnp.logical_and(rest > 0, jnp.logical_not(saturated())))
    def _():
        scores(2, 0, False)

        def cond(state):
            p, stop = state
            return jnp.logical_and(p < (rest + 2) // 2, stop == 0)

        def body(state):
            p, _ = state
            stop = saturated().astype(jnp.int32)
            step(2 * p + 1, 1)
            step(2 * p + 2, 0)
            return p + 1, stop

        lax.while_loop(cond, body, (jnp.int32(0), jnp.int32(0)))

    for g in range(nh // HEADS_PER_GROUP):
        out = acc_ref[HEADS_PER_GROUP * g]
        for j in range(1, HEADS_PER_GROUP):
            hh = HEADS_PER_GROUP * g + j
            out = jnp.where(own_lanes(hh), acc_ref[hh], out)
        o_ref[0, :, group(HEADS_PER_GROUP * g)] = out.astype(o_ref.dtype)


def _stick_breaking(q, kt, v):
    bsz, s, wb = q.shape
    t = SB_BLOCK
    nh = SB_HEADS
    width = nh * HEAD_DIM
    return pl.pallas_call(
        _sb_kernel,
        grid=(bsz, wb // width, s // t),
        in_specs=[
            pl.BlockSpec((1, t, width), lambda b, p, i: (b, i, p)),
            pl.BlockSpec((1, width, s), lambda b, p, i: (b, p, 0)),
            pl.BlockSpec((1, s, width), lambda b, p, i: (b, 0, p)),
        ],
        out_specs=pl.BlockSpec((1, t, width), lambda b, p, i: (b, i, p)),
        out_shape=jax.ShapeDtypeStruct((bsz, s, wb), BF16),
        scratch_shapes=[
            pltpu.VMEM((nh, t, LANES), F32),
            pltpu.VMEM((nh, t, LANES), F32),
            pltpu.VMEM((2 * nh, t, LANES), F32),
            pltpu.VMEM((2 * nh, t, t), BF16),
            pltpu.VMEM((2 * nh, t, t), F32),
            pltpu.VMEM((2 * nh, t, t), BF16),
        ],
        compiler_params=_params("parallel", "parallel", "parallel"),
        name="stick_breaking",
    )(q, kt, v)


HALO = 8


def _sc_kernel(x_ref, sh_ref, sc_ref, g_ref, ng_ref, win_ref, cw_ref, wout_ref, o_ref,
               cx_ref, z_ref):
    x = x_ref[0]
    tm, d = x.shape

    @pl.when(pl.program_id(1) == 0)
    def _():
        cx_ref[0:HALO, :] = jnp.zeros((HALO, d), F32)

    h = _modulated(x, ng_ref[0], sc_ref[0], sh_ref[0]).astype(BF16)
    cw = cw_ref[0]
    for c0 in range(0, d, FF_CHUNK):
        sl = slice(c0, c0 + FF_CHUNK)
        b_gate = jnp.dot(h, win_ref[:, sl], preferred_element_type=F32)
        c_gate = jnp.dot(h, win_ref[:, d + c0:d + c0 + FF_CHUNK], preferred_element_type=F32)
        xs = jnp.dot(h, win_ref[:, 2 * d + c0:2 * d + c0 + FF_CHUNK],
                     preferred_element_type=F32)
        cx_ref[HALO:HALO + tm, sl] = c_gate * xs
        y = (cw[2:3, sl] * cx_ref[HALO:HALO + tm, sl]
             + cw[1:2, sl] * cx_ref[HALO - 1:HALO - 1 + tm, sl]
             + cw[0:1, sl] * cx_ref[HALO - 2:HALO - 2 + tm, sl])
        z_ref[:, sl] = (b_gate * y).astype(BF16)
    cx_ref[0:HALO, :] = cx_ref[tm:tm + HALO, :]
    m = jnp.dot(z_ref[...], wout_ref[...], preferred_element_type=F32)
    o_ref[0] = x + g_ref[0] * m


def _short_conv(x, mod3, ng3, win, conv_w, wout, *, layer, sub, idx):
    bsz, s, d = x.shape
    tm = TOKEN_TILE
    return pl.pallas_call(
        _sc_kernel,
        grid=(bsz, s // tm),
        in_specs=[
            pl.BlockSpec((1, tm, d), lambda b, i: (b, i, 0)),
            _mod_spec(layer, 3 * sub + 0, d),
            _mod_spec(layer, 3 * sub + 1, d),
            _mod_spec(layer, 3 * sub + 2, d),
            pl.BlockSpec((1, 1, d), lambda b, i: (layer * 3 + sub, 0, 0)),
            _resident(win.shape, lambda b, i: (0, 0)),
            pl.BlockSpec((1,) + conv_w.shape[1:], lambda b, i: (idx, 0, 0)),
            _resident(wout.shape, lambda b, i: (0, 0)),
        ],
        out_specs=pl.BlockSpec((1, tm, d), lambda b, i: (b, i, 0)),
        out_shape=jax.ShapeDtypeStruct((bsz, s, d), F32),
        scratch_shapes=[pltpu.VMEM((tm + HALO, d), F32),
                        pltpu.VMEM((tm, d), BF16)],
        compiler_params=_params("parallel", "arbitrary"),
        name="short_conv",
    )(x, mod3, mod3, mod3, ng3, win, conv_w, wout)


def kernel(x, c, mod_w, mod_b, norm_g, ffn_w_gate, ffn_w_up, ffn_w_down, hy_w_in, hy_w_out,
           gm_vnorm_g, gm_w_s, gm_b_s, sc_w_in, sc_conv_w, sc_w_out, final_norm_g):
    depth = mod_w.shape[0]
    assert depth == 2, "layer 0's mixer hosts the weight casts of every later FFN"
    d = x.shape[-1]
    d_ff = ffn_w_gate.shape[-1]
    n_sub = norm_g.shape[1]

    mod3 = _modulation(c, mod_w, mod_b)
    ng3 = norm_g.reshape(depth * n_sub, 1, d)

    ffn_f32 = [w.reshape(-1, w.shape[-1]) for w in (ffn_w_gate, ffn_w_up, ffn_w_down)]
    ffn_rows = (d, d, d_ff)
    ffn_w = [(*[w[0, 0].astype(BF16) for w in (ffn_w_gate, ffn_w_up, ffn_w_down)], 0)]
    later_f32 = ([(w, r, r) for w, r in zip(ffn_f32, ffn_rows)]
                 + [(w, 2 * r, 2 * r) for w, r in zip(ffn_f32, ffn_rows)]
                 + [(w[0], 0, w.shape[1]) for w in (hy_w_out, sc_w_in, sc_w_out)])

    for layer in range(depth):
        i = layer // 2
        x = _ffn(x, mod3, ng3, ffn_w[2 * layer], final_norm_g, layer=layer, sub=0, final=False)
        if layer % 2 == 0:
            n_ha = gm_w_s.shape[1]
            wa_n = 2 * n_ha * HEAD_DIM
            wb = (hy_w_in.shape[2] - wa_n) // 3
            w_in = hy_w_in[i].astype(BF16)
            wkt = w_in[:, wa_n + wb:wa_n + 2 * wb].T
            head_of = jnp.arange(wa_n // 2) // HEAD_DIM
            gmat = (head_of[:, None] == head_of[None, :]).astype(BF16)
            ws2 = gm_w_s[i].astype(BF16).reshape(n_ha // 2, 2 * CHUNK, CHUNK)
            bs_tile = jnp.repeat(gm_b_s[i].T, HEAD_DIM, axis=1)
            ya, q, kt, v, *cast = _hy_in(x, mod3, ng3, w_in, wkt, gm_vnorm_g[i][None, :],
                                         gmat, ws2, bs_tile, later_f32, layer=layer, sub=1)
            ffn_w += [(*cast[0:3], 0), (*cast[3:6], 0), (*cast[3:6], 1)]
            hy_wo, sc_wi, sc_wo = cast[6:9]
            yb = _stick_breaking(q, kt, v)
            mix = (ya, yb, hy_wo)
        else:
            x = _short_conv(x, mod3, ng3, sc_wi, sc_conv_w, sc_wo, layer=layer, sub=1, idx=i)
            mix = None
        x = _ffn(x, mod3, ng3, ffn_w[2 * layer + 1], final_norm_g,
                 layer=layer, sub=2, final=(layer == depth - 1), mix=mix)
    return x
```

```python
import functools

import jax
import jax.numpy as jnp
from jax import lax
from jax.experimental import pallas as pl
from jax.experimental.pallas import tpu as pltpu

F32 = jnp.float32
BF16 = jnp.bfloat16

HEAD_DIM = 64
CHUNK = 128
EPS = 1e-6
MACARON_WEIGHT = 0.5
N_MOD = 9
MOD_ROWS = 8

LANES = 128
BF16_ROWS = 16
SB_BLOCK = 256
TOKEN_TILE = 1024
VMEM_LIMIT = 56 * 1024 * 1024


def _resident(block_shape, index_map):
    return pl.BlockSpec(block_shape, index_map, pipeline_mode=pl.Buffered(1))


def _params(*semantics):
    return pltpu.CompilerParams(dimension_semantics=semantics,
                                vmem_limit_bytes=VMEM_LIMIT)


def _rms(x):
    return lax.rsqrt(jnp.mean(x * x, axis=-1, keepdims=True) + EPS)


def _modulated(x, ng, sc, sh):
    return (x * _rms(x)) * (ng * (1.0 + sc)) + sh


def _silu(x):
    return x * jax.nn.sigmoid(x)


def _mod_kernel(c_ref, w_ref, b_ref, o_ref):
    cond = _silu(c_ref[...]).astype(BF16)
    res = jnp.dot(cond, w_ref[0].astype(BF16), preferred_element_type=F32) + b_ref[0]
    for b in range(MOD_ROWS):
        o_ref[b] = res[b:b + 1, :]


def _modulation(c, mod_w, mod_b):
    depth, d, n = mod_w.shape
    bsz = c.shape[0]
    assert n == N_MOD * d
    c8 = jnp.zeros((MOD_ROWS, d), F32).at[:bsz].set(c)
    return pl.pallas_call(
        _mod_kernel,
        grid=(depth, N_MOD),
        in_specs=[
            pl.BlockSpec((MOD_ROWS, d), lambda l, j: (0, 0)),
            pl.BlockSpec((1, d, d), lambda l, j: (l, 0, j)),
            pl.BlockSpec((1, 1, d), lambda l, j: (l, 0, j)),
        ],
        out_specs=pl.BlockSpec((MOD_ROWS, 1, d), lambda l, j: (l * N_MOD + j, 0, 0)),
        out_shape=jax.ShapeDtypeStruct((depth * N_MOD * MOD_ROWS, 1, d), F32),
        compiler_params=_params("parallel", "parallel"),
        name="adaln_mod",
    )(c8, mod_w, mod_b.reshape(depth, 1, n))


def _mod_spec(layer, k, d):
    base = (layer * N_MOD + k) * MOD_ROWS
    return pl.BlockSpec((1, 1, d), lambda b, i: (base + b, 0, 0))


FF_CHUNK = 256


def _ffn_kernel(*refs, final, mixed):
    if mixed:
        x_ref, ya_ref, yb_ref, wo_ref, gm_ref = refs[:5]
        wa = ya_ref.shape[2]
        m = (jnp.dot(ya_ref[0], wo_ref[:wa, :], preferred_element_type=F32)
             + jnp.dot(yb_ref[0], wo_ref[wa:, :], preferred_element_type=F32))
        x = x_ref[0] + gm_ref[0] * m
        refs = refs[5:]
    else:
        x = refs[0][0]
        refs = refs[1:]
    sh_ref, sc_ref, g_ref, ng_ref, wg_ref, wu_ref, wd_ref, fg_ref, o_ref, a_ref = refs
    h = _modulated(x, ng_ref[0], sc_ref[0], sh_ref[0]).astype(BF16)
    d_ff = a_ref.shape[1]
    for c0 in range(0, d_ff, FF_CHUNK):
        sl = slice(c0, c0 + FF_CHUNK)
        g = jnp.dot(h, wg_ref[:, sl], preferred_element_type=F32)
        u = jnp.dot(h, wu_ref[:, sl], preferred_element_type=F32)
        a_ref[:, sl] = (_silu(g) * u).astype(BF16)
    y = jnp.dot(a_ref[...], wd_ref[...], preferred_element_type=F32)
    out = x + (MACARON_WEIGHT * g_ref[0]) * y
    if final:
        out = (out * _rms(out)) * fg_ref[...]
    o_ref[0] = out


def _ffn(x, mod3, ng3, weights, final_g, *, layer, sub, final, mix=None):
    bsz, s, d = x.shape
    wg, wu, wd, f = weights
    d_ff = wg.shape[-1]
    tm = TOKEN_TILE
    tile = lambda b, i: (b, i, 0)
    mix_specs, mix_args = [], []
    if mix is not None:
        ya, yb, wo = mix
        mix_specs = [
            pl.BlockSpec((1, tm, ya.shape[2]), tile),
            pl.BlockSpec((1, tm, yb.shape[2]), tile),
            _resident(wo.shape, lambda b, i: (0, 0)),
            _mod_spec(layer, 3 * (sub - 1) + 2, d),
        ]
        mix_args = [ya, yb, wo, mod3]
    return pl.pallas_call(
        functools.partial(_ffn_kernel, final=final, mixed=mix is not None),
        grid=(bsz, s // tm),
        in_specs=[
            pl.BlockSpec((1, tm, d), tile),
            *mix_specs,
            _mod_spec(layer, 3 * sub + 0, d),
            _mod_spec(layer, 3 * sub + 1, d),
            _mod_spec(layer, 3 * sub + 2, d),
            pl.BlockSpec((1, 1, d), lambda b, i: (layer * 3 + sub, 0, 0)),
            _resident((d, d_ff), lambda b, i: (f, 0)),
            _resident((d, d_ff), lambda b, i: (f, 0)),
            _resident((d_ff, d), lambda b, i: (f, 0)),
            pl.BlockSpec((1, d), lambda b, i: (0, 0)),
        ],
        out_specs=pl.BlockSpec((1, tm, d), tile),
        out_shape=jax.ShapeDtypeStruct((bsz, s, d), F32),
        scratch_shapes=[pltpu.VMEM((tm, d_ff), BF16)],
        compiler_params=_params("parallel", "parallel"),
        name=f"ffn_l{layer}_s{sub}",
    )(x, *mix_args, mod3, mod3, mod3, ng3, wg, wu, wd, final_g.reshape(1, d))


def _hyin_kernel(*refs, n_cast):
    (x_ref, sh_ref, sc_ref, ng_ref, wa_ref, wq_ref, wkt_ref, wv_ref, vg_ref,
     gmat_ref, ws_ref, bs_ref) = refs[:12]
    cast_in = refs[12:12 + n_cast]
    ya_ref, q_ref, kt_ref, v_ref = refs[12 + n_cast:16 + n_cast]
    cast_out = refs[16 + n_cast:]
    for src, dst in zip(cast_in, cast_out):
        dst[...] = src[...].astype(BF16)

    x = x_ref[0]
    tm = x.shape[0]
    h = _modulated(x, ng_ref[0], sc_ref[0], sh_ref[0]).astype(BF16)

    uv = jnp.dot(h, wa_ref[...], preferred_element_type=F32)
    q = jnp.dot(h, wq_ref[...], preferred_element_type=F32)
    q_ref[0] = (q * (HEAD_DIM ** -0.5)).astype(BF16)
    uv = 0.5 * uv * (1.0 + lax.erf(uv * (2.0 ** -0.5)))
    wa = uv.shape[1] // 2
    u, v = uv[:, :wa], uv[:, wa:]
    ssq = jnp.dot((v * v).astype(BF16), gmat_ref[...], preferred_element_type=F32)
    v_ref[0] = jnp.dot(h, wv_ref[...], preferred_element_type=F32).astype(BF16)
    vn = (v * lax.rsqrt(ssq * (1.0 / HEAD_DIM) + EPS) * vg_ref[...]).astype(BF16)

    n_pairs = wa // LANES
    r2 = lax.broadcasted_iota(jnp.int32, (2 * CHUNK, CHUNK), 0)
    c2 = lax.broadcasted_iota(jnp.int32, (2 * CHUNK, CHUNK), 1)
    causal = (r2 & (CHUNK - 1)) >= c2
    first_head = lax.broadcasted_iota(jnp.int32, (CHUNK, LANES), 1) < HEAD_DIM
    w_pairs = [jnp.where(causal, ws_ref[p], jnp.zeros_like(ws_ref[p])) for p in range(n_pairs)]
    for ci in range(0, tm // CHUNK, 2):
        rows = [slice((ci + j) * CHUNK, (ci + j + 1) * CHUNK) for j in range(2)]
        parts = [[], []]
        for p in range(n_pairs):
            lanes = slice(p * LANES, (p + 1) * LANES)
            vp = jnp.concatenate([vn[rows[0], lanes], vn[rows[1], lanes]], axis=1)
            r = jnp.dot(w_pairs[p], vp, preferred_element_type=F32)
            for j in range(2):
                rj = r[:, j * LANES:(j + 1) * LANES]
                parts[j].append(jnp.where(first_head, rj[:CHUNK], rj[CHUNK:]))
        for j in range(2):
            sv = jnp.concatenate(parts[j], axis=1) + bs_ref[...]
            ya_ref[0, rows[j], :] = (u[rows[j]] * sv).astype(BF16)
    kt_ref[0] = lax.dot_general(wkt_ref[...], h, (((1,), (1,)), ((), ())),
                                preferred_element_type=F32).astype(BF16)


def _hy_in(x, mod3, ng3, w_in, wkt, vg, gmat, ws2, bs_tile, cast, *, layer, sub):
    bsz, s, d = x.shape
    tm = TOKEN_TILE
    n_i = s // tm
    wb = wkt.shape[0]
    wa_n = w_in.shape[1] - 3 * wb
    assert wa_n % wb == 0
    const2 = lambda b, i: (0, 0)

    def cast_specs(w, start, count):
        rb = count // (bsz * n_i)
        assert rb * bsz * n_i == count and rb % BF16_ROWS == 0 and start % rb == 0
        cols = w.shape[1]
        return (pl.BlockSpec((rb, cols), lambda b, i: (start // rb + b * n_i + i, 0)),
                pl.BlockSpec((rb, cols), lambda b, i: (b * n_i + i, 0)),
                jax.ShapeDtypeStruct((count, cols), BF16))

    casts = [cast_specs(*c) for c in cast]
    return pl.pallas_call(
        functools.partial(_hyin_kernel, n_cast=len(casts)),
        grid=(bsz, n_i),
        in_specs=[
            pl.BlockSpec((1, tm, d), lambda b, i: (b, i, 0)),
            _mod_spec(layer, 3 * sub + 0, d),
            _mod_spec(layer, 3 * sub + 1, d),
            pl.BlockSpec((1, 1, d), lambda b, i: (layer * 3 + sub, 0, 0)),
            _resident((d, wa_n), const2),
            _resident((d, wb), lambda b, i: (0, wa_n // wb)),
            _resident(wkt.shape, const2),
            _resident((d, wb), lambda b, i: (0, wa_n // wb + 2)),
            pl.BlockSpec(vg.shape, const2),
            _resident(gmat.shape, const2),
            _resident(ws2.shape, lambda b, i: (0, 0, 0)),
            pl.BlockSpec(bs_tile.shape, const2),
            *[c[0] for c in casts],
        ],
        out_specs=[
            pl.BlockSpec((1, tm, wa_n // 2), lambda b, i: (b, i, 0)),
            pl.BlockSpec((1, tm, wb), lambda b, i: (b, i, 0)),
            pl.BlockSpec((1, wb, tm), lambda b, i: (b, 0, i)),
            pl.BlockSpec((1, tm, wb), lambda b, i: (b, i, 0)),
            *[c[1] for c in casts],
        ],
        out_shape=[
            jax.ShapeDtypeStruct((bsz, s, wa_n // 2), BF16),
            jax.ShapeDtypeStruct((bsz, s, wb), BF16),
            jax.ShapeDtypeStruct((bsz, wb, s), BF16),
            jax.ShapeDtypeStruct((bsz, s, wb), BF16),
            *[c[2] for c in casts],
        ],
        compiler_params=_params("parallel", "parallel"),
        name="hy_in",
    )(x, mod3, mod3, ng3, w_in, w_in, wkt, w_in, vg, gmat, ws2, bs_tile, *[c[0] for c in cast])


MASKED_LOG = -1e30
LOG2E = 1.4426950408889634
SATURATED = 105.0
SB_HEADS = 8
HEADS_PER_GROUP = LANES // HEAD_DIM


def _sb_kernel(q_ref, kt_ref, v_ref, o_ref,
               acc_ref, c_ref, cb_ref, sp_ref, lb_ref, w_ref):
    t = q_ref.shape[1]
    nh = acc_ref.shape[0]
    qi = pl.program_id(2)
    lane = lax.broadcasted_iota(jnp.int32, (t, LANES), 1)
    row = lax.broadcasted_iota(jnp.int32, (t, t), 0)
    col = lax.broadcasted_iota(jnp.int32, (t, t), 1)
    later = (row > col).astype(BF16)
    before = col < row
    ones = jnp.ones((t, LANES), BF16)

    def group(hh):
        g = hh // HEADS_PER_GROUP
        return slice(g * LANES, (g + 1) * LANES)

    def own_lanes(hh):
        return (lane // HEAD_DIM) == (hh % HEADS_PER_GROUP)

    q_heads = []
    for hh in range(nh):
        qg = q_ref[0, :, group(hh)]
        q_heads.append(jnp.where(own_lanes(hh), qg, jnp.zeros_like(qg)))

    def scores(k, dst, diag):
        k0 = pl.multiple_of((qi - k) * t, t)
        for hh in range(nh):
            kt = kt_ref[0, group(hh), pl.ds(k0, t)]
            z = jnp.dot(q_heads[hh], kt, preferred_element_type=F32)
            if diag:
                z = jnp.where(before, z, MASKED_LOG)
            e = jnp.exp2(jnp.abs(z) * (-LOG2E))
            sp = jnp.maximum(z, 0.0) + jnp.log(1.0 + e)
            log_b = z - sp
            sp_ref[nh * dst + hh] = sp.astype(BF16)
            lb_ref[nh * dst + hh] = log_b
            cb_ref[nh * dst + hh] = c_ref[hh]
            c_ref[hh] += jnp.dot(sp.astype(BF16), ones, preferred_element_type=F32)

    def weights(src, dst):
        for hh in range(nh):
            tail = jnp.dot(sp_ref[nh * src + hh], later, preferred_element_type=F32)
            cb = cb_ref[nh * src + hh]
            logw = (lb_ref[nh * src + hh] - tail) - jnp.concatenate([cb] * (t // LANES), axis=1)
            w_ref[nh * dst + hh] = jnp.exp(logw).astype(BF16)

    def values(k, src, valid=None):
        k0 = pl.multiple_of((qi - k) * t, t)
        for hh in range(nh):
            vv = v_ref[0, pl.ds(k0, t), group(hh)]
            if valid is not None:
                vv = jnp.where(valid, vv, jnp.zeros_like(vv))
            acc_ref[hh] += jnp.dot(w_ref[nh * src + hh], vv, preferred_element_type=F32)

    def saturated():
        return jnp.min(c_ref[...]) >= SATURATED

    acc_ref[...] = jnp.zeros_like(acc_ref)
    c_ref[...] = jnp.zeros_like(c_ref)

    second = jnp.minimum(qi, 1)
    scores(0, 0, True)
    scores(second, 1, False)
    weights(0, 0)
    values(0, 0)
    weights(1, 1)
    values(second, 1, qi >= 1)

    rest = qi - 1

    def step(it, new):
        old = 1 - new
        scores(jnp.minimum(2 + it, qi), new, False)
        weights(old, new)
        r = it - 2
        values(jnp.clip(2 + r, 2, qi), old, jnp.logical_and(r >= 0, r < rest))

    @pl.when(jnp.logical_and(rest > 0, jnp.logical_not(saturated())))
    def _():
        scores(2, 0, False)

        def cond(state):
            p, stop = state
            return jnp.logical_and(p < (rest + 2) // 2, stop == 0)

        def body(state):
            p, _ = state
            stop = saturated().astype(jnp.int32)
            step(2 * p + 1, 1)
            step(2 * p + 2, 0)
            return p + 1, stop

        lax.while_loop(cond, body, (jnp.int32(0), jnp.int32(0)))

    for g in range(nh // HEADS_PER_GROUP):
        out = acc_ref[HEADS_PER_GROUP * g]
        for j in range(1, HEADS_PER_GROUP):
            hh = HEADS_PER_GROUP * g + j
            out = jnp.where(own_lanes(hh), acc_ref[hh], out)
        o_ref[0, :, group(HEADS_PER_GROUP * g)] = out.astype(o_ref.dtype)


def _stick_breaking(q, kt, v):
    bsz, s, wb = q.shape
    t = SB_BLOCK
    nh = SB_HEADS
    width = nh * HEAD_DIM
    return pl.pallas_call(
        _sb_kernel,
        grid=(bsz, wb // width, s // t),
        in_specs=[
            pl.BlockSpec((1, t, width), lambda b, p, i: (b, i, p)),
            pl.BlockSpec((1, width, s), lambda b, p, i: (b, p, 0)),
            pl.BlockSpec((1, s, width), lambda b, p, i: (b, 0, p)),
        ],
        out_specs=pl.BlockSpec((1, t, width), lambda b, p, i: (b, i, p)),
        out_shape=jax.ShapeDtypeStruct((bsz, s, wb), BF16),
        scratch_shapes=[
            pltpu.VMEM((nh, t, LANES), F32),
            pltpu.VMEM((nh, t, LANES), F32),
            pltpu.VMEM((2 * nh, t, LANES), F32),
            pltpu.VMEM((2 * nh, t, t), BF16),
            pltpu.VMEM((2 * nh, t, t), F32),
            pltpu.VMEM((2 * nh, t, t), BF16),
        ],
        compiler_params=_params("parallel", "parallel", "parallel"),
        name="stick_breaking",
    )(q, kt, v)


HALO = 8


def _sc_kernel(x_ref, sh_ref, sc_ref, g_ref, ng_ref, win_ref, cw_ref, wout_ref, o_ref,
               cx_ref, z_ref):
    x = x_ref[0]
    tm, d = x.shape

    @pl.when(pl.program_id(1) == 0)
    def _():
        cx_ref[0:HALO, :] = jnp.zeros((HALO, d), F32)

    h = _modulated(x, ng_ref[0], sc_ref[0], sh_ref[0]).astype(BF16)
    cw = cw_ref[0]
    for c0 in range(0, d, FF_CHUNK):
        sl = slice(c0, c0 + FF_CHUNK)
        b_gate = jnp.dot(h, win_ref[:, sl], preferred_element_type=F32)
        c_gate = jnp.dot(h, win_ref[:, d + c0:d + c0 + FF_CHUNK], preferred_element_type=F32)
        xs = jnp.dot(h, win_ref[:, 2 * d + c0:2 * d + c0 + FF_CHUNK],
                     preferred_element_type=F32)
        cx_ref[HALO:HALO + tm, sl] = c_gate * xs
        y = (cw[2:3, sl] * cx_ref[HALO:HALO + tm, sl]
             + cw[1:2, sl] * cx_ref[HALO - 1:HALO - 1 + tm, sl]
             + cw[0:1, sl] * cx_ref[HALO - 2:HALO - 2 + tm, sl])
        z_ref[:, sl] = (b_gate * y).astype(BF16)
    cx_ref[0:HALO, :] = cx_ref[tm:tm + HALO, :]
    m = jnp.dot(z_ref[...], wout_ref[...], preferred_element_type=F32)
    o_ref[0] = x + g_ref[0] * m


def _short_conv(x, mod3, ng3, win, conv_w, wout, *, layer, sub, idx):
    bsz, s, d = x.shape
    tm = TOKEN_TILE
    return pl.pallas_call(
        _sc_kernel,
        grid=(bsz, s // tm),
        in_specs=[
            pl.BlockSpec((1, tm, d), lambda b, i: (b, i, 0)),
            _mod_spec(layer, 3 * sub + 0, d),
            _mod_spec(layer, 3 * sub + 1, d),
            _mod_spec(layer, 3 * sub + 2, d),
            pl.BlockSpec((1, 1, d), lambda b, i: (layer * 3 + sub, 0, 0)),
            _resident(win.shape, lambda b, i: (0, 0)),
            pl.BlockSpec((1,) + conv_w.shape[1:], lambda b, i: (idx, 0, 0)),
            _resident(wout.shape, lambda b, i: (0, 0)),
        ],
        out_specs=pl.BlockSpec((1, tm, d), lambda b, i: (b, i, 0)),
        out_shape=jax.ShapeDtypeStruct((bsz, s, d), F32),
        scratch_shapes=[pltpu.VMEM((tm + HALO, d), F32),
                        pltpu.VMEM((tm, d), BF16)],
        compiler_params=_params("parallel", "arbitrary"),
        name="short_conv",
    )(x, mod3, mod3, mod3, ng3, win, conv_w, wout)


def kernel(x, c, mod_w, mod_b, norm_g, ffn_w_gate, ffn_w_up, ffn_w_down, hy_w_in, hy_w_out,
           gm_vnorm_g, gm_w_s, gm_b_s, sc_w_in, sc_conv_w, sc_w_out, final_norm_g):
    depth = mod_w.shape[0]
    assert depth == 2, "layer 0's mixer hosts the weight casts of every later FFN"
    d = x.shape[-1]
    d_ff = ffn_w_gate.shape[-1]
    n_sub = norm_g.shape[1]

    mod3 = _modulation(c, mod_w, mod_b)
    ng3 = norm_g.reshape(depth * n_sub, 1, d)

    ffn_f32 = [w.reshape(-1, w.shape[-1]) for w in (ffn_w_gate, ffn_w_up, ffn_w_down)]
    ffn_rows = (d, d, d_ff)
    ffn_w = [(*[w[0, 0].astype(BF16) for w in (ffn_w_gate, ffn_w_up, ffn_w_down)], 0)]
    later_f32 = ([(w, r, r) for w, r in zip(ffn_f32, ffn_rows)]
                 + [(w, 2 * r, 2 * r) for w, r in zip(ffn_f32, ffn_rows)]
                 + [(w[0], 0, w.shape[1]) for w in (hy_w_out, sc_w_in, sc_w_out)])

    for layer in range(depth):
        i = layer // 2
        x = _ffn(x, mod3, ng3, ffn_w[2 * layer], final_norm_g, layer=layer, sub=0, final=False)
        if layer % 2 == 0:
            n_ha = gm_w_s.shape[1]
            wa_n = 2 * n_ha * HEAD_DIM
            wb = (hy_w_in.shape[2] - wa_n) // 3
            w_in = hy_w_in[i].astype(BF16)
            wkt = w_in[:, wa_n + wb:wa_n + 2 * wb].T
            head_of = jnp.arange(wa_n // 2) // HEAD_DIM
            gmat = (head_of[:, None] == head_of[None, :]).astype(BF16)
            ws2 = gm_w_s[i].astype(BF16).reshape(n_ha // 2, 2 * CHUNK, CHUNK)
            bs_tile = jnp.repeat(gm_b_s[i].T, HEAD_DIM, axis=1)
            ya, q, kt, v, *cast = _hy_in(x, mod3, ng3, w_in, wkt, gm_vnorm_g[i][None, :],
                                         gmat, ws2, bs_tile, later_f32, layer=layer, sub=1)
            ffn_w += [(*cast[0:3], 0), (*cast[3:6], 0), (*cast[3:6], 1)]
            hy_wo, sc_wi, sc_wo = cast[6:9]
            yb = _stick_breaking(q, kt, v)
            mix = (ya, yb, hy_wo)
        else:
            x = _short_conv(x, mod3, ng3, sc_wi, sc_conv_w, sc_wo, layer=layer, sub=1, idx=i)
            mix = None
        x = _ffn(x, mod3, ng3, ffn_w[2 * layer + 1], final_norm_g,
                 layer=layer, sub=2, final=(layer == depth - 1), mix=mix)
    return x
```

```python
import functools

import jax
import jax.numpy as jnp
from jax import lax
from jax.experimental import pallas as pl
from jax.experimental.pallas import tpu as pltpu

F32 = jnp.float32
BF16 = jnp.bfloat16

HEAD_DIM = 64
CHUNK = 128
EPS = 1e-6
MACARON_WEIGHT = 0.5
N_MOD = 9
MOD_ROWS = 8

LANES = 128
BF16_ROWS = 16
SB_BLOCK = 256
TOKEN_TILE = 1024
VMEM_LIMIT = 56 * 1024 * 1024


def _resident(block_shape, index_map):
    return pl.BlockSpec(block_shape, index_map, pipeline_mode=pl.Buffered(1))


def _params(*semantics):
    return pltpu.CompilerParams(dimension_semantics=semantics,
                                vmem_limit_bytes=VMEM_LIMIT)


def _rms(x):
    return lax.rsqrt(jnp.mean(x * x, axis=-1, keepdims=True) + EPS)


def _modulated(x, ng, sc, sh):
    return (x * _rms(x)) * (ng * (1.0 + sc)) + sh


def _silu(x):
    return x * jax.nn.sigmoid(x)


def _mod_kernel(c_ref, w_ref, b_ref, o_ref):
    cond = _silu(c_ref[...]).astype(BF16)
    res = jnp.dot(cond, w_ref[0].astype(BF16), preferred_element_type=F32) + b_ref[0]
    for b in range(MOD_ROWS):
        o_ref[b] = res[b:b + 1, :]


def _modulation(c, mod_w, mod_b):
    depth, d, n = mod_w.shape
    bsz = c.shape[0]
    assert n == N_MOD * d
    c8 = jnp.zeros((MOD_ROWS, d), F32).at[:bsz].set(c)
    return pl.pallas_call(
        _mod_kernel,
        grid=(depth, N_MOD),
        in_specs=[
            pl.BlockSpec((MOD_ROWS, d), lambda l, j: (0, 0)),
            pl.BlockSpec((1, d, d), lambda l, j: (l, 0, j)),
            pl.BlockSpec((1, 1, d), lambda l, j: (l, 0, j)),
        ],
        out_specs=pl.BlockSpec((MOD_ROWS, 1, d), lambda l, j: (l * N_MOD + j, 0, 0)),
        out_shape=jax.ShapeDtypeStruct((depth * N_MOD * MOD_ROWS, 1, d), F32),
        compiler_params=_params("parallel", "parallel"),
        name="adaln_mod",
    )(c8, mod_w, mod_b.reshape(depth, 1, n))


def _mod_spec(layer, k, d):
    base = (layer * N_MOD + k) * MOD_ROWS
    return pl.BlockSpec((1, 1, d), lambda b, i: (base + b, 0, 0))


FF_CHUNK = 256


def _ffn_kernel(*refs, final, mixed):
    if mixed:
        x_ref, ya_ref, yb_ref, wo_ref, gm_ref = refs[:5]
        wa = ya_ref.shape[2]
        m = (jnp.dot(ya_ref[0], wo_ref[:wa, :], preferred_element_type=F32)
             + jnp.dot(yb_ref[0], wo_ref[wa:, :], preferred_element_type=F32))
        x = x_ref[0] + gm_ref[0] * m
        refs = refs[5:]
    else:
        x = refs[0][0]
        refs = refs[1:]
    sh_ref, sc_ref, g_ref, ng_ref, wg_ref, wu_ref, wd_ref, fg_ref, o_ref, a_ref = refs
    h = _modulated(x, ng_ref[0], sc_ref[0], sh_ref[0]).astype(BF16)
    d_ff = a_ref.shape[1]
    for c0 in range(0, d_ff, FF_CHUNK):
        sl = slice(c0, c0 + FF_CHUNK)
        g = jnp.dot(h, wg_ref[:, sl], preferred_element_type=F32)
        u = jnp.dot(h, wu_ref[:, sl], preferred_element_type=F32)
        a_ref[:, sl] = (_silu(g) * u).astype(BF16)
    y = jnp.dot(a_ref[...], wd_ref[...], preferred_element_type=F32)
    out = x + (MACARON_WEIGHT * g_ref[0]) * y
    if final:
        out = (out * _rms(out)) * fg_ref[...]
    o_ref[0] = out


def _ffn(x, mod3, ng3, weights, final_g, *, layer, sub, final, mix=None):
    bsz, s, d = x.shape
    wg, wu, wd, f = weights
    d_ff = wg.shape[-1]
    tm = TOKEN_TILE
    tile = lambda b, i: (b, i, 0)
    mix_specs, mix_args = [], []
    if mix is not None:
        ya, yb, wo = mix
        mix_specs = [
            pl.BlockSpec((1, tm, ya.shape[2]), tile),
            pl.BlockSpec((1, tm, yb.shape[2]), tile),
            _resident(wo.shape, lambda b, i: (0, 0)),
            _mod_spec(layer, 3 * (sub - 1) + 2, d),
        ]
        mix_args = [ya, yb, wo, mod3]
    return pl.pallas_call(
        functools.partial(_ffn_kernel, final=final, mixed=mix is not None),
        grid=(bsz, s // tm),
        in_specs=[
            pl.BlockSpec((1, tm, d), tile),
            *mix_specs,
            _mod_spec(layer, 3 * sub + 0, d),
            _mod_spec(layer, 3 * sub + 1, d),
            _mod_spec(layer, 3 * sub + 2, d),
            pl.BlockSpec((1, 1, d), lambda b, i: (layer * 3 + sub, 0, 0)),
            _resident((d, d_ff), lambda b, i: (f, 0)),
            _resident((d, d_ff), lambda b, i: (f, 0)),
            _resident((d_ff, d), lambda b, i: (f, 0)),
            pl.BlockSpec((1, d), lambda b, i: (0, 0)),
        ],
        out_specs=pl.BlockSpec((1, tm, d), tile),
        out_shape=jax.ShapeDtypeStruct((bsz, s, d), F32),
        scratch_shapes=[pltpu.VMEM((tm, d_ff), BF16)],
        compiler_params=_params("parallel", "parallel"),
        name=f"ffn_l{layer}_s{sub}",
    )(x, *mix_args, mod3, mod3, mod3, ng3, wg, wu, wd, final_g.reshape(1, d))


def _hyin_kernel(*refs, n_cast):
    (x_ref, sh_ref, sc_ref, ng_ref, wa_ref, wq_ref, wkt_ref, wv_ref, vg_ref,
     gmat_ref, ws_ref, bs_ref) = refs[:12]
    cast_in = refs[12:12 + n_cast]
    ya_ref, q_ref, kt_ref, v_ref = refs[12 + n_cast:16 + n_cast]
    cast_out = refs[16 + n_cast:]
    for src, dst in zip(cast_in, cast_out):
        dst[...] = src[...].astype(BF16)

    x = x_ref[0]
    tm = x.shape[0]
    h = _modulated(x, ng_ref[0], sc_ref[0], sh_ref[0]).astype(BF16)

    uv = jnp.dot(h, wa_ref[...], preferred_element_type=F32)
    q = jnp.dot(h, wq_ref[...], preferred_element_type=F32)
    q_ref[0] = (q * (HEAD_DIM ** -0.5)).astype(BF16)
    uv = 0.5 * uv * (1.0 + lax.erf(uv * (2.0 ** -0.5)))
    wa = uv.shape[1] // 2
    u, v = uv[:, :wa], uv[:, wa:]
    ssq = jnp.dot((v * v).astype(BF16), gmat_ref[...], preferred_element_type=F32)
    v_ref[0] = jnp.dot(h, wv_ref[...], preferred_element_type=F32).astype(BF16)
    vn = (v * lax.rsqrt(ssq * (1.0 / HEAD_DIM) + EPS) * vg_ref[...]).astype(BF16)

    n_pairs = wa // LANES
    r2 = lax.broadcasted_iota(jnp.int32, (2 * CHUNK, CHUNK), 0)
    c2 = lax.broadcasted_iota(jnp.int32, (2 * CHUNK, CHUNK), 1)
    causal = (r2 & (CHUNK - 1)) >= c2
    first_head = lax.broadcasted_iota(jnp.int32, (CHUNK, LANES), 1) < HEAD_DIM
    w_pairs = [jnp.where(causal, ws_ref[p], jnp.zeros_like(ws_ref[p])) for p in range(n_pairs)]
    for ci in range(0, tm // CHUNK, 2):
        rows = [slice((ci + j) * CHUNK, (ci + j + 1) * CHUNK) for j in range(2)]
        parts = [[], []]
        for p in range(n_pairs):
            lanes = slice(p * LANES, (p + 1) * LANES)
            vp = jnp.concatenate([vn[rows[0], lanes], vn[rows[1], lanes]], axis=1)
            r = jnp.dot(w_pairs[p], vp, preferred_element_type=F32)
            for j in range(2):
                rj = r[:, j * LANES:(j + 1) * LANES]
                parts[j].append(jnp.where(first_head, rj[:CHUNK], rj[CHUNK:]))
        for j in range(2):
            sv = jnp.concatenate(parts[j], axis=1) + bs_ref[...]
            ya_ref[0, rows[j], :] = (u[rows[j]] * sv).astype(BF16)
    kt_ref[0] = lax.dot_general(wkt_ref[...], h, (((1,), (1,)), ((), ())),
                                preferred_element_type=F32).astype(BF16)


def _hy_in(x, mod3, ng3, w_in, wkt, vg, gmat, ws2, bs_tile, cast, *, layer, sub):
    bsz, s, d = x.shape
    tm = TOKEN_TILE
    n_i = s // tm
    wb = wkt.shape[0]
    wa_n = w_in.shape[1] - 3 * wb
    assert wa_n % wb == 0
    const2 = lambda b, i: (0, 0)

    def cast_specs(w, start, count):
        rb = count // (bsz * n_i)
        assert rb * bsz * n_i == count and rb % BF16_ROWS == 0 and start % rb == 0
        cols = w.shape[1]
        return (pl.BlockSpec((rb, cols), lambda b, i: (start // rb + b * n_i + i, 0)),
                pl.BlockSpec((rb, cols), lambda b, i: (b * n_i + i, 0)),
                jax.ShapeDtypeStruct((count, cols), BF16))

    casts = [cast_specs(*c) for c in cast]
    return pl.pallas_call(
        functools.partial(_hyin_kernel, n_cast=len(casts)),
        grid=(bsz, n_i),
        in_specs=[
            pl.BlockSpec((1, tm, d), lambda b, i: (b, i, 0)),
            _mod_spec(layer, 3 * sub + 0, d),
            _mod_spec(layer, 3 * sub + 1, d),
            pl.BlockSpec((1, 1, d), lambda b, i: (layer * 3 + sub, 0, 0)),
            _resident((d, wa_n), const2),
            _resident((d, wb), lambda b, i: (0, wa_n // wb)),
            _resident(wkt.shape, const2),
            _resident((d, wb), lambda b, i: (0, wa_n // wb + 2)),
            pl.BlockSpec(vg.shape, const2),
            _resident(gmat.shape, const2),
            _resident(ws2.shape, lambda b, i: (0, 0, 0)),
            pl.BlockSpec(bs_tile.shape, const2),
            *[c[0] for c in casts],
        ],
        out_specs=[
            pl.BlockSpec((1, tm, wa_n // 2), lambda b, i: (b, i, 0)),
            pl.BlockSpec((1, tm, wb), lambda b, i: (b, i, 0)),
            pl.BlockSpec((1, wb, tm), lambda b, i: (b, 0, i)),
            pl.BlockSpec((1, tm, wb), lambda b, i: (b, i, 0)),
            *[c[1] for c in casts],
        ],
        out_shape=[
            jax.ShapeDtypeStruct((bsz, s, wa_n // 2), BF16),
            jax.ShapeDtypeStruct((bsz, s, wb), BF16),
            jax.ShapeDtypeStruct((bsz, wb, s), BF16),
            jax.ShapeDtypeStruct((bsz, s, wb), BF16),
            *[c[2] for c in casts],
        ],
        compiler_params=_params("parallel", "parallel"),
        name="hy_in",
    )(x, mod3, mod3, ng3, w_in, w_in, wkt, w_in, vg, gmat, ws2, bs_tile, *[c[0] for c in cast])


MASKED_LOG = -1e30
LOG2E = 1.4426950408889634
SATURATED = 105.0
SB_HEADS = 8
HEADS_PER_GROUP = LANES // HEAD_DIM


def _sb_kernel(q_ref, kt_ref, v_ref, o_ref,
               acc_ref, c_ref, cb_ref, sp_ref, lb_ref, w_ref):
    t = q_ref.shape[1]
    nh = acc_ref.shape[0]
    qi = pl.program_id(2)
    lane = lax.broadcasted_iota(jnp.int32, (t, LANES), 1)
    row = lax.broadcasted_iota(jnp.int32, (t, t), 0)
    col = lax.broadcasted_iota(jnp.int32, (t, t), 1)
    later = (row > col).astype(BF16)
    before = col < row
    ones = jnp.ones((t, LANES), BF16)

    def group(hh):
        g = hh // HEADS_PER_GROUP
        return slice(g * LANES, (g + 1) * LANES)

    def own_lanes(hh):
        return (lane // HEAD_DIM) == (hh % HEADS_PER_GROUP)

    q_heads = []
    for hh in range(nh):
        qg = q_ref[0, :, group(hh)]
        q_heads.append(jnp.where(own_lanes(hh), qg, jnp.zeros_like(qg)))

    def scores(k, dst, diag):
        k0 = pl.multiple_of((qi - k) * t, t)
        for hh in range(nh):
            kt = kt_ref[0, group(hh), pl.ds(k0, t)]
            z = jnp.dot(q_heads[hh], kt, preferred_element_type=F32)
            if diag:
                z = jnp.where(before, z, MASKED_LOG)
            e = jnp.exp2(jnp.abs(z) * (-LOG2E))
            sp = jnp.maximum(z, 0.0) + jnp.log(1.0 + e)
            log_b = z - sp
            sp_ref[nh * dst + hh] = sp.astype(BF16)
            lb_ref[nh * dst + hh] = log_b
            row_sums = jnp.dot(sp.astype(BF16), ones, preferred_element_type=F32)
            if diag:
                cb_ref[nh * dst + hh] = jnp.zeros_like(row_sums)
                c_ref[hh] = row_sums
            else:
                cb_ref[nh * dst + hh] = c_ref[hh]
                c_ref[hh] += row_sums

    def weights(src, dst):
        for hh in range(nh):
            tail = jnp.dot(sp_ref[nh * src + hh], later, preferred_element_type=F32)
            cb = cb_ref[nh * src + hh]
            logw = (lb_ref[nh * src + hh] - tail) - jnp.concatenate([cb] * (t // LANES), axis=1)
            w_ref[nh * dst + hh] = jnp.exp(logw).astype(BF16)

    def values(k, src, valid=None, assign=False):
        k0 = pl.multiple_of((qi - k) * t, t)
        for hh in range(nh):
            vv = v_ref[0, pl.ds(k0, t), group(hh)]
            if valid is not None:
                vv = jnp.where(valid, vv, jnp.zeros_like(vv))
            prod = jnp.dot(w_ref[nh * src + hh], vv, preferred_element_type=F32)
            if assign:
                acc_ref[hh] = prod
            else:
                acc_ref[hh] += prod

    def saturated():
        return jnp.min(c_ref[...]) >= SATURATED

    second = jnp.minimum(qi, 1)
    scores(0, 0, True)
    scores(second, 1, False)
    weights(0, 0)
    values(0, 0, assign=True)
    weights(1, 1)
    values(second, 1, qi >= 1)

    rest = qi - 1

    def step(it, new):
        old = 1 - new
        scores(jnp.minimum(2 + it, qi), new, False)
        weights(old, new)
        r = it - 2
        values(jnp.clip(2 + r, 2, qi), old, jnp.logical_and(r >= 0, r < rest))

    @pl.when(jnp.logical_and(rest > 0, jnp.logical_not(saturated())))
    def _():
        scores(2, 0, False)

        def cond(state):
            p, stop = state
            return jnp.logical_and(p < (rest + 2) // 2, stop == 0)

        def body(state):
            p, _ = state
            stop = saturated().astype(jnp.int32)
            step(2 * p + 1, 1)
            step(2 * p + 2, 0)
            return p + 1, stop

        lax.while_loop(cond, body, (jnp.int32(0), jnp.int32(0)))

    for g in range(nh // HEADS_PER_GROUP):
        out = acc_ref[HEADS_PER_GROUP * g]
        for j in range(1, HEADS_PER_GROUP):
            hh = HEADS_PER_GROUP * g + j
            out = jnp.where(own_lanes(hh), acc_ref[hh], out)
        o_ref[0, :, group(HEADS_PER_GROUP * g)] = out.astype(o_ref.dtype)


def _stick_breaking(q, kt, v):
    bsz, s, wb = q.shape
    t = SB_BLOCK
    nh = SB_HEADS
    width = nh * HEAD_DIM
    return pl.pallas_call(
        _sb_kernel,
        grid=(bsz, wb // width, s // t),
        in_specs=[
            pl.BlockSpec((1, t, width), lambda b, p, i: (b, i, p)),
            pl.BlockSpec((1, width, s), lambda b, p, i: (b, p, 0)),
            pl.BlockSpec((1, s, width), lambda b, p, i: (b, 0, p)),
        ],
        out_specs=pl.BlockSpec((1, t, width), lambda b, p, i: (b, i, p)),
        out_shape=jax.ShapeDtypeStruct((bsz, s, wb), BF16),
        scratch_shapes=[
            pltpu.VMEM((nh, t, LANES), F32),
            pltpu.VMEM((nh, t, LANES), F32),
            pltpu.VMEM((2 * nh, t, LANES), F32),
            pltpu.VMEM((2 * nh, t, t), BF16),
            pltpu.VMEM((2 * nh, t, t), F32),
            pltpu.VMEM((2 * nh, t, t), BF16),
        ],
        compiler_params=_params("parallel", "parallel", "parallel"),
        name="stick_breaking",
    )(q, kt, v)


HALO = 8


def _sc_kernel(x_ref, sh_ref, sc_ref, g_ref, ng_ref, win_ref, cw_ref, wout_ref, o_ref,
               cx_ref, z_ref):
    x = x_ref[0]
    tm, d = x.shape

    @pl.when(pl.program_id(1) == 0)
    def _():
        cx_ref[0:HALO, :] = jnp.zeros((HALO, d), F32)

    h = _modulated(x, ng_ref[0], sc_ref[0], sh_ref[0]).astype(BF16)
    cw = cw_ref[0]
    for c0 in range(0, d, FF_CHUNK):
        sl = slice(c0, c0 + FF_CHUNK)
        b_gate = jnp.dot(h, win_ref[:, sl], preferred_element_type=F32)
        c_gate = jnp.dot(h, win_ref[:, d + c0:d + c0 + FF_CHUNK], preferred_element_type=F32)
        xs = jnp.dot(h, win_ref[:, 2 * d + c0:2 * d + c0 + FF_CHUNK],
                     preferred_element_type=F32)
        cx_ref[HALO:HALO + tm, sl] = c_gate * xs
        y = (cw[2:3, sl] * cx_ref[HALO:HALO + tm, sl]
             + cw[1:2, sl] * cx_ref[HALO - 1:HALO - 1 + tm, sl]
             + cw[0:1, sl] * cx_ref[HALO - 2:HALO - 2 + tm, sl])
        z_ref[:, sl] = (b_gate * y).astype(BF16)
    cx_ref[0:HALO, :] = cx_ref[tm:tm + HALO, :]
    m = jnp.dot(z_ref[...], wout_ref[...], preferred_element_type=F32)
    o_ref[0] = x + g_ref[0] * m


def _short_conv(x, mod3, ng3, win, conv_w, wout, *, layer, sub, idx):
    bsz, s, d = x.shape
    tm = TOKEN_TILE
    return pl.pallas_call(
        _sc_kernel,
        grid=(bsz, s // tm),
        in_specs=[
            pl.BlockSpec((1, tm, d), lambda b, i: (b, i, 0)),
            _mod_spec(layer, 3 * sub + 0, d),
            _mod_spec(layer, 3 * sub + 1, d),
            _mod_spec(layer, 3 * sub + 2, d),
            pl.BlockSpec((1, 1, d), lambda b, i: (layer * 3 + sub, 0, 0)),
            _resident(win.shape, lambda b, i: (0, 0)),
            pl.BlockSpec((1,) + conv_w.shape[1:], lambda b, i: (idx, 0, 0)),
            _resident(wout.shape, lambda b, i: (0, 0)),
        ],
        out_specs=pl.BlockSpec((1, tm, d), lambda b, i: (b, i, 0)),
        out_shape=jax.ShapeDtypeStruct((bsz, s, d), F32),
        scratch_shapes=[pltpu.VMEM((tm + HALO, d), F32),
                        pltpu.VMEM((tm, d), BF16)],
        compiler_params=_params("parallel", "arbitrary"),
        name="short_conv",
    )(x, mod3, mod3, mod3, ng3, win, conv_w, wout)


def kernel(x, c, mod_w, mod_b, norm_g, ffn_w_gate, ffn_w_up, ffn_w_down, hy_w_in, hy_w_out,
           gm_vnorm_g, gm_w_s, gm_b_s, sc_w_in, sc_conv_w, sc_w_out, final_norm_g):
    depth = mod_w.shape[0]
    assert depth == 2, "layer 0's mixer hosts the weight casts of every later FFN"
    d = x.shape[-1]
    d_ff = ffn_w_gate.shape[-1]
    n_sub = norm_g.shape[1]

    mod3 = _modulation(c, mod_w, mod_b)
    ng3 = norm_g.reshape(depth * n_sub, 1, d)

    ffn_f32 = [w.reshape(-1, w.shape[-1]) for w in (ffn_w_gate, ffn_w_up, ffn_w_down)]
    ffn_rows = (d, d, d_ff)
    ffn_w = [(*[w[0, 0].astype(BF16) for w in (ffn_w_gate, ffn_w_up, ffn_w_down)], 0)]
    later_f32 = ([(w, r, r) for w, r in zip(ffn_f32, ffn_rows)]
                 + [(w, 2 * r, 2 * r) for w, r in zip(ffn_f32, ffn_rows)]
                 + [(w[0], 0, w.shape[1]) for w in (hy_w_out, sc_w_in, sc_w_out)])

    for layer in range(depth):
        i = layer // 2
        x = _ffn(x, mod3, ng3, ffn_w[2 * layer], final_norm_g, layer=layer, sub=0, final=False)
        if layer % 2 == 0:
            n_ha = gm_w_s.shape[1]
            wa_n = 2 * n_ha * HEAD_DIM
            wb = (hy_w_in.shape[2] - wa_n) // 3
            w_in = hy_w_in[i].astype(BF16)
            wkt = w_in[:, wa_n + wb:wa_n + 2 * wb].T
            head_of = jnp.arange(wa_n // 2) // HEAD_DIM
            gmat = (head_of[:, None] == head_of[None, :]).astype(BF16)
            ws2 = gm_w_s[i].astype(BF16).reshape(n_ha // 2, 2 * CHUNK, CHUNK)
            bs_tile = jnp.repeat(gm_b_s[i].T, HEAD_DIM, axis=1)
            ya, q, kt, v, *cast = _hy_in(x, mod3, ng3, w_in, wkt, gm_vnorm_g[i][None, :],
                                         gmat, ws2, bs_tile, later_f32, layer=layer, sub=1)
            ffn_w += [(*cast[0:3], 0), (*cast[3:6], 0), (*cast[3:6], 1)]
            hy_wo, sc_wi, sc_wo = cast[6:9]
            yb = _stick_breaking(q, kt, v)
            mix = (ya, yb, hy_wo)
        else:
            x = _short_conv(x, mod3, ng3, sc_wi, sc_conv_w, sc_wo, layer=layer, sub=1, idx=i)
            mix = None
        x = _ffn(x, mod3, ng3, ffn_w[2 * layer + 1], final_norm_g,
                 layer=layer, sub=2, final=(layer == depth - 1), mix=mix)
    return x
```

```python
import functools

import jax
import jax.numpy as jnp
from jax import lax
from jax.experimental import pallas as pl
from jax.experimental.pallas import tpu as pltpu

F32 = jnp.float32
BF16 = jnp.bfloat16

HEAD_DIM = 64
CHUNK = 128
EPS = 1e-6
MACARON_WEIGHT = 0.5
N_MOD = 9
MOD_ROWS = 8

LANES = 128
BF16_ROWS = 16
SB_BLOCK = 256
TOKEN_TILE = 1024
VMEM_LIMIT = 56 * 1024 * 1024


def _resident(block_shape, index_map):
    return pl.BlockSpec(block_shape, index_map, pipeline_mode=pl.Buffered(1))


def _params(*semantics):
    return pltpu.CompilerParams(dimension_semantics=semantics,
                                vmem_limit_bytes=VMEM_LIMIT)


def _rms(x):
    return lax.rsqrt(jnp.mean(x * x, axis=-1, keepdims=True) + EPS)


def _modulated(x, ng, sc, sh):
    return (x * _rms(x)) * (ng * (1.0 + sc)) + sh


def _silu(x):
    return x * jax.nn.sigmoid(x)


def _mod_kernel(c_ref, w_ref, b_ref, o_ref):
    cond = _silu(c_ref[...]).astype(BF16)
    res = jnp.dot(cond, w_ref[0].astype(BF16), preferred_element_type=F32) + b_ref[0]
    for b in range(MOD_ROWS):
        o_ref[b] = res[b:b + 1, :]


def _modulation(c, mod_w, mod_b):
    depth, d, n = mod_w.shape
    bsz = c.shape[0]
    assert n == N_MOD * d
    c8 = jnp.zeros((MOD_ROWS, d), F32).at[:bsz].set(c)
    return pl.pallas_call(
        _mod_kernel,
        grid=(depth, N_MOD),
        in_specs=[
            pl.BlockSpec((MOD_ROWS, d), lambda l, j: (0, 0)),
            pl.BlockSpec((1, d, d), lambda l, j: (l, 0, j)),
            pl.BlockSpec((1, 1, d), lambda l, j: (l, 0, j)),
        ],
        out_specs=pl.BlockSpec((MOD_ROWS, 1, d), lambda l, j: (l * N_MOD + j, 0, 0)),
        out_shape=jax.ShapeDtypeStruct((depth * N_MOD * MOD_ROWS, 1, d), F32),
        compiler_params=_params("parallel", "parallel"),
        name="adaln_mod",
    )(c8, mod_w, mod_b.reshape(depth, 1, n))


def _mod_spec(layer, k, d):
    base = (layer * N_MOD + k) * MOD_ROWS
    return pl.BlockSpec((1, 1, d), lambda b, i: (base + b, 0, 0))


FF_CHUNK = 256


def _ffn_kernel(*refs, final, mixed):
    if mixed:
        x_ref, ya_ref, yb_ref, wo_ref, gm_ref = refs[:5]
        wa = ya_ref.shape[2]
        m = (jnp.dot(ya_ref[0], wo_ref[:wa, :], preferred_element_type=F32)
             + jnp.dot(yb_ref[0], wo_ref[wa:, :], preferred_element_type=F32))
        x = x_ref[0] + gm_ref[0] * m
        refs = refs[5:]
    else:
        x = refs[0][0]
        refs = refs[1:]
    sh_ref, sc_ref, g_ref, ng_ref, wg_ref, wu_ref, wd_ref, fg_ref, o_ref, a_ref = refs
    h = _modulated(x, ng_ref[0], sc_ref[0], sh_ref[0]).astype(BF16)
    d_ff = a_ref.shape[1]
    for c0 in range(0, d_ff, FF_CHUNK):
        sl = slice(c0, c0 + FF_CHUNK)
        g = jnp.dot(h, wg_ref[:, sl], preferred_element_type=F32)
        u = jnp.dot(h, wu_ref[:, sl], preferred_element_type=F32)
        a_ref[:, sl] = (_silu(g) * u).astype(BF16)
    y = jnp.dot(a_ref[...], wd_ref[...], preferred_element_type=F32)
    out = x + (MACARON_WEIGHT * g_ref[0]) * y
    if final:
        out = (out * _rms(out)) * fg_ref[...]
    o_ref[0] = out


def _ffn(x, mod3, ng3, weights, final_g, *, layer, sub, final, mix=None):
    bsz, s, d = x.shape
    wg, wu, wd, f = weights
    d_ff = wg.shape[-1]
    tm = TOKEN_TILE
    tile = lambda b, i: (b, i, 0)
    mix_specs, mix_args = [], []
    if mix is not None:
        ya, yb, wo = mix
        mix_specs = [
            pl.BlockSpec((1, tm, ya.shape[2]), tile),
            pl.BlockSpec((1, tm, yb.shape[2]), tile),
            _resident(wo.shape, lambda b, i: (0, 0)),
            _mod_spec(layer, 3 * (sub - 1) + 2, d),
        ]
        mix_args = [ya, yb, wo, mod3]
    return pl.pallas_call(
        functools.partial(_ffn_kernel, final=final, mixed=mix is not None),
        grid=(bsz, s // tm),
        in_specs=[
            pl.BlockSpec((1, tm, d), tile),
            *mix_specs,
            _mod_spec(layer, 3 * sub + 0, d),
            _mod_spec(layer, 3 * sub + 1, d),
            _mod_spec(layer, 3 * sub + 2, d),
            pl.BlockSpec((1, 1, d), lambda b, i: (layer * 3 + sub, 0, 0)),
            _resident((d, d_ff), lambda b, i: (f, 0)),
            _resident((d, d_ff), lambda b, i: (f, 0)),
            _resident((d_ff, d), lambda b, i: (f, 0)),
            pl.BlockSpec((1, d), lambda b, i: (0, 0)),
        ],
        out_specs=pl.BlockSpec((1, tm, d), tile),
        out_shape=jax.ShapeDtypeStruct((bsz, s, d), F32),
        scratch_shapes=[pltpu.VMEM((tm, d_ff), BF16)],
        compiler_params=_params("parallel", "parallel"),
        name=f"ffn_l{layer}_s{sub}",
    )(x, *mix_args, mod3, mod3, mod3, ng3, wg, wu, wd, final_g.reshape(1, d))


def _hyin_kernel(*refs, n_cast):
    (x_ref, sh_ref, sc_ref, ng_ref, wa_ref, wq_ref, wkt_ref, wv_ref, vg_ref,
     gmat_ref, ws_ref, bs_ref) = refs[:12]
    cast_in = refs[12:12 + n_cast]
    ya_ref, q_ref, kt_ref, v_ref = refs[12 + n_cast:16 + n_cast]
    cast_out = refs[16 + n_cast:]
    for src, dst in zip(cast_in, cast_out):
        dst[...] = src[...].astype(BF16)

    x = x_ref[0]
    tm = x.shape[0]
    h = _modulated(x, ng_ref[0], sc_ref[0], sh_ref[0]).astype(BF16)

    uv = jnp.dot(h, wa_ref[...], preferred_element_type=F32)
    q = jnp.dot(h, wq_ref[...], preferred_element_type=F32)
    q_ref[0] = (q * (HEAD_DIM ** -0.5)).astype(BF16)
    uv = 0.5 * uv * (1.0 + lax.erf(uv * (2.0 ** -0.5)))
    wa = uv.shape[1] // 2
    u, v = uv[:, :wa], uv[:, wa:]
    ssq = jnp.dot((v * v).astype(BF16), gmat_ref[...], preferred_element_type=F32)
    v_ref[0] = jnp.dot(h, wv_ref[...], preferred_element_type=F32).astype(BF16)
    vn = (v * lax.rsqrt(ssq * (1.0 / HEAD_DIM) + EPS) * vg_ref[...]).astype(BF16)

    n_pairs = wa // LANES
    r2 = lax.broadcasted_iota(jnp.int32, (2 * CHUNK, CHUNK), 0)
    c2 = lax.broadcasted_iota(jnp.int32, (2 * CHUNK, CHUNK), 1)
    causal = (r2 & (CHUNK - 1)) >= c2
    first_head = lax.broadcasted_iota(jnp.int32, (CHUNK, LANES), 1) < HEAD_DIM
    w_pairs = [jnp.where(causal, ws_ref[p], jnp.zeros_like(ws_ref[p])) for p in range(n_pairs)]
    for ci in range(0, tm // CHUNK, 2):
        rows = [slice((ci + j) * CHUNK, (ci + j + 1) * CHUNK) for j in range(2)]
        parts = [[], []]
        for p in range(n_pairs):
            lanes = slice(p * LANES, (p + 1) * LANES)
            vp = jnp.concatenate([vn[rows[0], lanes], vn[rows[1], lanes]], axis=1)
            r = jnp.dot(w_pairs[p], vp, preferred_element_type=F32)
            for j in range(2):
                rj = r[:, j * LANES:(j + 1) * LANES]
                parts[j].append(jnp.where(first_head, rj[:CHUNK], rj[CHUNK:]))
        for j in range(2):
            sv = jnp.concatenate(parts[j], axis=1) + bs_ref[...]
            ya_ref[0, rows[j], :] = (u[rows[j]] * sv).astype(BF16)
    kt_ref[0] = lax.dot_general(wkt_ref[...], h, (((1,), (1,)), ((), ())),
                                preferred_element_type=F32).astype(BF16)


def _hy_in(x, mod3, ng3, w_in, wkt, vg, gmat, ws2, bs_tile, cast, *, layer, sub):
    bsz, s, d = x.shape
    tm = TOKEN_TILE
    n_i = s // tm
    wb = wkt.shape[0]
    wa_n = w_in.shape[1] - 3 * wb
    assert wa_n % wb == 0
    const2 = lambda b, i: (0, 0)

    def cast_specs(w, start, count):
        rb = count // (bsz * n_i)
        assert rb * bsz * n_i == count and rb % BF16_ROWS == 0 and start % rb == 0
        cols = w.shape[1]
        return (pl.BlockSpec((rb, cols), lambda b, i: (start // rb + b * n_i + i, 0)),
                pl.BlockSpec((rb, cols), lambda b, i: (b * n_i + i, 0)),
                jax.ShapeDtypeStruct((count, cols), BF16))

    casts = [cast_specs(*c) for c in cast]
    return pl.pallas_call(
        functools.partial(_hyin_kernel, n_cast=len(casts)),
        grid=(bsz, n_i),
        in_specs=[
            pl.BlockSpec((1, tm, d), lambda b, i: (b, i, 0)),
            _mod_spec(layer, 3 * sub + 0, d),
            _mod_spec(layer, 3 * sub + 1, d),
            pl.BlockSpec((1, 1, d), lambda b, i: (layer * 3 + sub, 0, 0)),
            _resident((d, wa_n), const2),
            _resident((d, wb), lambda b, i: (0, wa_n // wb)),
            _resident(wkt.shape, const2),
            _resident((d, wb), lambda b, i: (0, wa_n // wb + 2)),
            pl.BlockSpec(vg.shape, const2),
            _resident(gmat.shape, const2),
            _resident(ws2.shape, lambda b, i: (0, 0, 0)),
            pl.BlockSpec(bs_tile.shape, const2),
            *[c[0] for c in casts],
        ],
        out_specs=[
            pl.BlockSpec((1, tm, wa_n // 2), lambda b, i: (b, i, 0)),
            pl.BlockSpec((1, tm, wb), lambda b, i: (b, i, 0)),
            pl.BlockSpec((1, wb, tm), lambda b, i: (b, 0, i)),
            pl.BlockSpec((1, tm, wb), lambda b, i: (b, i, 0)),
            *[c[1] for c in casts],
        ],
        out_shape=[
            jax.ShapeDtypeStruct((bsz, s, wa_n // 2), BF16),
            jax.ShapeDtypeStruct((bsz, s, wb), BF16),
            jax.ShapeDtypeStruct((bsz, wb, s), BF16),
            jax.ShapeDtypeStruct((bsz, s, wb), BF16),
            *[c[2] for c in casts],
        ],
        compiler_params=_params("parallel", "parallel"),
        name="hy_in",
    )(x, mod3, mod3, ng3, w_in, w_in, wkt, w_in, vg, gmat, ws2, bs_tile, *[c[0] for c in cast])


MASKED_LOG = -1e30
LOG2E = 1.4426950408889634
SATURATED = 105.0
SB_HEADS = 4
HEADS_PER_GROUP = LANES // HEAD_DIM


def _sb_kernel(q_ref, kt_ref, v_ref, o_ref,
               acc_ref, c_ref, cb_ref, sp_ref, lb_ref, w_ref):
    t = q_ref.shape[1]
    nh = acc_ref.shape[0]
    qi = pl.program_id(2)
    lane = lax.broadcasted_iota(jnp.int32, (t, LANES), 1)
    row = lax.broadcasted_iota(jnp.int32, (t, t), 0)
    col = lax.broadcasted_iota(jnp.int32, (t, t), 1)
    later = (row > col).astype(BF16)
    before = col < row
    ones = jnp.ones((t, LANES), BF16)

    def group(hh):
        g = hh // HEADS_PER_GROUP
        return slice(g * LANES, (g + 1) * LANES)

    def own_lanes(hh):
        return (lane // HEAD_DIM) == (hh % HEADS_PER_GROUP)

    q_heads = []
    for hh in range(nh):
        qg = q_ref[0, :, group(hh)]
        q_heads.append(jnp.where(own_lanes(hh), qg, jnp.zeros_like(qg)))

    def scores(k, dst, diag):
        k0 = pl.multiple_of((qi - k) * t, t)
        for hh in range(nh):
            kt = kt_ref[0, group(hh), pl.ds(k0, t)]
            z = jnp.dot(q_heads[hh], kt, preferred_element_type=F32)
            if diag:
                z = jnp.where(before, z, MASKED_LOG)
            e = jnp.exp2(jnp.abs(z) * (-LOG2E))
            sp = jnp.maximum(z, 0.0) + jnp.log(1.0 + e)
            log_b = z - sp
            sp_ref[nh * dst + hh] = sp.astype(BF16)
            lb_ref[nh * dst + hh] = log_b
            row_sums = jnp.dot(sp.astype(BF16), ones, preferred_element_type=F32)
            if diag:
                cb_ref[nh * dst + hh] = jnp.zeros_like(row_sums)
                c_ref[hh] = row_sums
            else:
                cb_ref[nh * dst + hh] = c_ref[hh]
                c_ref[hh] += row_sums

    def weights(src, dst):
        for hh in range(nh):
            tail = jnp.dot(sp_ref[nh * src + hh], later, preferred_element_type=F32)
            cb = cb_ref[nh * src + hh]
            logw = (lb_ref[nh * src + hh] - tail) - jnp.concatenate([cb] * (t // LANES), axis=1)
            w_ref[nh * dst + hh] = jnp.exp(logw).astype(BF16)

    def values(k, src, valid=None, assign=False):
        k0 = pl.multiple_of((qi - k) * t, t)
        for hh in range(nh):
            vv = v_ref[0, pl.ds(k0, t), group(hh)]
            if valid is not None:
                vv = jnp.where(valid, vv, jnp.zeros_like(vv))
            prod = jnp.dot(w_ref[nh * src + hh], vv, preferred_element_type=F32)
            if assign:
                acc_ref[hh] = prod
            else:
                acc_ref[hh] += prod

    def saturated():
        return jnp.min(c_ref[...]) >= SATURATED

    second = jnp.minimum(qi, 1)
    scores(0, 0, True)
    scores(second, 1, False)
    weights(0, 0)
    values(0, 0, assign=True)
    weights(1, 1)
    values(second, 1, qi >= 1)

    rest = qi - 1

    def step(it, new):
        old = 1 - new
        scores(jnp.minimum(2 + it, qi), new, False)
        weights(old, new)
        r = it - 2
        values(jnp.clip(2 + r, 2, qi), old, jnp.logical_and(r >= 0, r < rest))

    @pl.when(jnp.logical_and(rest > 0, jnp.logical_not(saturated())))
    def _():
        scores(2, 0, False)

        def cond(state):
            p, stop = state
            return jnp.logical_and(p < (rest + 2) // 2, stop == 0)

        def body(state):
            p, _ = state
            stop = saturated().astype(jnp.int32)
            step(2 * p + 1, 1)
            step(2 * p + 2, 0)
            return p + 1, stop

        lax.while_loop(cond, body, (jnp.int32(0), jnp.int32(0)))

    for g in range(nh // HEADS_PER_GROUP):
        out = acc_ref[HEADS_PER_GROUP * g]
        for j in range(1, HEADS_PER_GROUP):
            hh = HEADS_PER_GROUP * g + j
            out = jnp.where(own_lanes(hh), acc_ref[hh], out)
        o_ref[0, :, group(HEADS_PER_GROUP * g)] = out.astype(o_ref.dtype)


def _stick_breaking(q, kt, v):
    bsz, s, wb = q.shape
    t = SB_BLOCK
    nh = SB_HEADS
    width = nh * HEAD_DIM
    return pl.pallas_call(
        _sb_kernel,
        grid=(bsz, wb // width, s // t),
        in_specs=[
            pl.BlockSpec((1, t, width), lambda b, p, i: (b, i, p)),
            pl.BlockSpec((1, width, s), lambda b, p, i: (b, p, 0)),
            pl.BlockSpec((1, s, width), lambda b, p, i: (b, 0, p)),
        ],
        out_specs=pl.BlockSpec((1, t, width), lambda b, p, i: (b, i, p)),
        out_shape=jax.ShapeDtypeStruct((bsz, s, wb), BF16),
        scratch_shapes=[
            pltpu.VMEM((nh, t, LANES), F32),
            pltpu.VMEM((nh, t, LANES), F32),
            pltpu.VMEM((2 * nh, t, LANES), F32),
            pltpu.VMEM((2 * nh, t, t), BF16),
            pltpu.VMEM((2 * nh, t, t), F32),
            pltpu.VMEM((2 * nh, t, t), BF16),
        ],
        compiler_params=_params("parallel", "parallel", "parallel"),
        name="stick_breaking",
    )(q, kt, v)


HALO = 8


def _sc_kernel(x_ref, sh_ref, sc_ref, g_ref, ng_ref, win_ref, cw_ref, wout_ref, o_ref,
               cx_ref, z_ref):
    x = x_ref[0]
    tm, d = x.shape

    @pl.when(pl.program_id(1) == 0)
    def _():
        cx_ref[0:HALO, :] = jnp.zeros((HALO, d), F32)

    h = _modulated(x, ng_ref[0], sc_ref[0], sh_ref[0]).astype(BF16)
    cw = cw_ref[0]
    for c0 in range(0, d, FF_CHUNK):
        sl = slice(c0, c0 + FF_CHUNK)
        b_gate = jnp.dot(h, win_ref[:, sl], preferred_element_type=F32)
        c_gate = jnp.dot(h, win_ref[:, d + c0:d + c0 + FF_CHUNK], preferred_element_type=F32)
        xs = jnp.dot(h, win_ref[:, 2 * d + c0:2 * d + c0 + FF_CHUNK],
                     preferred_element_type=F32)
        cx_ref[HALO:HALO + tm, sl] = c_gate * xs
        y = (cw[2:3, sl] * cx_ref[HALO:HALO + tm, sl]
             + cw[1:2, sl] * cx_ref[HALO - 1:HALO - 1 + tm, sl]
             + cw[0:1, sl] * cx_ref[HALO - 2:HALO - 2 + tm, sl])
        z_ref[:, sl] = (b_gate * y).astype(BF16)
    cx_ref[0:HALO, :] = cx_ref[tm:tm + HALO, :]
    m = jnp.dot(z_ref[...], wout_ref[...], preferred_element_type=F32)
    o_ref[0] = x + g_ref[0] * m


def _short_conv(x, mod3, ng3, win, conv_w, wout, *, layer, sub, idx):
    bsz, s, d = x.shape
    tm = TOKEN_TILE
    return pl.pallas_call(
        _sc_kernel,
        grid=(bsz, s // tm),
        in_specs=[
            pl.BlockSpec((1, tm, d), lambda b, i: (b, i, 0)),
            _mod_spec(layer, 3 * sub + 0, d),
            _mod_spec(layer, 3 * sub + 1, d),
            _mod_spec(layer, 3 * sub + 2, d),
            pl.BlockSpec((1, 1, d), lambda b, i: (layer * 3 + sub, 0, 0)),
            _resident(win.shape, lambda b, i: (0, 0)),
            pl.BlockSpec((1,) + conv_w.shape[1:], lambda b, i: (idx, 0, 0)),
            _resident(wout.shape, lambda b, i: (0, 0)),
        ],
        out_specs=pl.BlockSpec((1, tm, d), lambda b, i: (b, i, 0)),
        out_shape=jax.ShapeDtypeStruct((bsz, s, d), F32),
        scratch_shapes=[pltpu.VMEM((tm + HALO, d), F32),
                        pltpu.VMEM((tm, d), BF16)],
        compiler_params=_params("parallel", "arbitrary"),
        name="short_conv",
    )(x, mod3, mod3, mod3, ng3, win, conv_w, wout)


def kernel(x, c, mod_w, mod_b, norm_g, ffn_w_gate, ffn_w_up, ffn_w_down, hy_w_in, hy_w_out,
           gm_vnorm_g, gm_w_s, gm_b_s, sc_w_in, sc_conv_w, sc_w_out, final_norm_g):
    depth = mod_w.shape[0]
    assert depth == 2, "layer 0's mixer hosts the weight casts of every later FFN"
    d = x.shape[-1]
    d_ff = ffn_w_gate.shape[-1]
    n_sub = norm_g.shape[1]

    mod3 = _modulation(c, mod_w, mod_b)
    ng3 = norm_g.reshape(depth * n_sub, 1, d)

    ffn_f32 = [w.reshape(-1, w.shape[-1]) for w in (ffn_w_gate, ffn_w_up, ffn_w_down)]
    ffn_rows = (d, d, d_ff)
    ffn_w = [(*[w[0, 0].astype(BF16) for w in (ffn_w_gate, ffn_w_up, ffn_w_down)], 0)]
    later_f32 = ([(w, r, r) for w, r in zip(ffn_f32, ffn_rows)]
                 + [(w, 2 * r, 2 * r) for w, r in zip(ffn_f32, ffn_rows)]
                 + [(w[0], 0, w.shape[1]) for w in (hy_w_out, sc_w_in, sc_w_out)])

    for layer in range(depth):
        i = layer // 2
        x = _ffn(x, mod3, ng3, ffn_w[2 * layer], final_norm_g, layer=layer, sub=0, final=False)
        if layer % 2 == 0:
            n_ha = gm_w_s.shape[1]
            wa_n = 2 * n_ha * HEAD_DIM
            wb = (hy_w_in.shape[2] - wa_n) // 3
            w_in = hy_w_in[i].astype(BF16)
            wkt = w_in[:, wa_n + wb:wa_n + 2 * wb].T
            head_of = jnp.arange(wa_n // 2) // HEAD_DIM
            gmat = (head_of[:, None] == head_of[None, :]).astype(BF16)
            ws2 = gm_w_s[i].astype(BF16).reshape(n_ha // 2, 2 * CHUNK, CHUNK)
            bs_tile = jnp.repeat(gm_b_s[i].T, HEAD_DIM, axis=1)
            ya, q, kt, v, *cast = _hy_in(x, mod3, ng3, w_in, wkt, gm_vnorm_g[i][None, :],
                                         gmat, ws2, bs_tile, later_f32, layer=layer, sub=1)
            ffn_w += [(*cast[0:3], 0), (*cast[3:6], 0), (*cast[3:6], 1)]
            hy_wo, sc_wi, sc_wo = cast[6:9]
            yb = _stick_breaking(q, kt, v)
            mix = (ya, yb, hy_wo)
        else:
            x = _short_conv(x, mod3, ng3, sc_wi, sc_conv_w, sc_wo, layer=layer, sub=1, idx=i)
            mix = None
        x = _ffn(x, mod3, ng3, ffn_w[2 * layer + 1], final_norm_g,
                 layer=layer, sub=2, final=(layer == depth - 1), mix=mix)
    return x
```
